```python
import jax, jax.numpy as jnp
from jax import lax
import numpy as np

D_MODEL = 1024
BATCH = 8
SEQ = 2048
DEPTH = 2

CHUNK = 64
N_META = 16
META_PAD = CHUNK - N_META
HEAD_DIM = 64
A_HEADS = 8
IDX_HEADS = 8
IDX_DIM = 64
DSA_MAX_K = 256
A_QBLK = 64
B_HEADS = 8
B_VDIM = 2 * HEAD_DIM
C_HEADS = 8
C_KV_HEADS = 2
C_WINDOW = 128
C_WIN_CHUNKS = C_WINDOW // CHUNK
D_WIDTH = 512
POOL_WINDOWS = (2, 4, 8, 16)
D_GROUP = D_WIDTH // len(POOL_WINDOWS)
D_FF = 2816
EPS = 1e-6
NEG = -1e30

EVEN_SIZES = ([A_HEADS * HEAD_DIM] * 3 + [IDX_HEADS * IDX_DIM, IDX_DIM, IDX_HEADS]
              + [B_HEADS * HEAD_DIM] * 2 + [B_HEADS * B_VDIM] * 2)
EVEN_IN = sum(EVEN_SIZES)
EVEN_OUT = A_HEADS * HEAD_DIM + B_HEADS * B_VDIM
ODD_SIZES = [C_HEADS * HEAD_DIM, C_KV_HEADS * HEAD_DIM, C_KV_HEADS * HEAD_DIM, D_WIDTH]
ODD_IN = sum(ODD_SIZES)
ODD_OUT = C_HEADS * HEAD_DIM + D_WIDTH
N_EVEN = (DEPTH + 1) // 2
N_ODD = DEPTH // 2

kernel_name = "hybrid_streaming_dsa_retention_swa_pool"


def rms_norm(x, gain=None):
    xf = x.astype(jnp.float32)
    y = xf * lax.rsqrt(jnp.mean(xf * xf, axis=-1, keepdims=True) + EPS)
    if gain is not None:
        y = y * gain.astype(jnp.float32)
    return y.astype(x.dtype)


def swiglu(x, w_in, w_out):
    g, u = jnp.split(x @ w_in, 2, axis=-1)
    return (jax.nn.silu(g) * u) @ w_out


def split_cols(a, sizes):
    out, off = [], 0
    for s in sizes:
        out.append(a[..., off:off + s])
        off += s
    return out


def chunk_ids(n):
    return (jnp.arange(n) + META_PAD) // CHUNK


def pad_left(a):
    return jnp.pad(a, [(0, 0), (META_PAD, 0)] + [(0, 0)] * (a.ndim - 2))


def dsa_attention(q, k, v, iq, ik, iw, top_k):
    B, L, H, dh = q.shape
    cid = chunk_ids(L)
    nb = -(-L // A_QBLK)
    padn = nb * A_QBLK - L

    def blocks(a):
        a = jnp.pad(a, [(0, 0), (0, padn)] + [(0, 0)] * (a.ndim - 2))
        return jnp.moveaxis(a.reshape((B, nb, A_QBLK) + a.shape[2:]), 1, 0)

    q_cid = chunk_ids(nb * A_QBLK).reshape(nb, A_QBLK)
    iw = iw * (IDX_HEADS ** -0.5 * IDX_DIM ** -0.5)
    gather = jax.vmap(lambda t, i: t[i])

    def one_block(args):
        qb, iqb, iwb, qc = args
        logits = jnp.einsum('bqhd,bsd->bqhs', iqb, ik)
        score = jnp.einsum('bqh,bqhs->bqs', iwb, jax.nn.relu(logits)).astype(jnp.float32)
        adm = cid[None, :] <= qc[:, None]
        score = jnp.where(adm[None], score, NEG)
        _, idx = lax.top_k(score, top_k)
        valid = cid[idx] <= qc[None, :, None]
        kg = gather(k, idx)
        vg = gather(v, idx)
        s = jnp.einsum('bqhd,bqkhd->bhqk', qb, kg).astype(jnp.float32) * dh ** -0.5
        s = jnp.where(valid[:, None], s, NEG)
        p = jax.nn.softmax(s, axis=-1).astype(v.dtype)
        return jnp.einsum('bhqk,bqkhd->bqhd', p, vg)

    out = lax.map(one_block, (blocks(q), blocks(iq), blocks(iw), q_cid))
    return jnp.moveaxis(out, 0, 1).reshape(B, nb * A_QBLK, H, dh)[:, :L]


def rotary(x, pos):
    d = x.shape[-1]
    inv = 1.0 / (10000.0 ** (jnp.arange(0, d, 2, dtype=jnp.float32) / d))
    ang = pos.astype(jnp.float32)[:, None] * inv[None]
    cos = jnp.cos(ang)[None, :, None, :]
    sin = jnp.sin(ang)[None, :, None, :]
    x1, x2 = jnp.split(x.astype(jnp.float32), 2, axis=-1)
    return jnp.concatenate([x1 * cos - x2 * sin, x1 * sin + x2 * cos], axis=-1).astype(x.dtype)


def retention(q, k, v):
    B, Lp, H, dk = q.shape
    dv = v.shape[-1]
    N = Lp // CHUNK
    pos = jnp.arange(Lp) - META_PAD
    q = rotary(q, pos)
    k = rotary(k, pos) * dk ** -0.5
    log_g = jnp.log(1.0 - 2.0 ** (-5.0 - jnp.arange(H, dtype=jnp.float32)))
    i = jnp.arange(CHUNK, dtype=jnp.float32)
    rel = i[:, None] - i[None, :]
    dmat = jnp.where(rel >= 0, jnp.exp(jnp.maximum(rel, 0.0)[None] * log_g[:, None, None]), 0.0)
    q_dec = jnp.exp((i + 1.0)[:, None] * log_g[None])
    k_dec = jnp.exp((CHUNK - 1.0 - i)[:, None] * log_g[None])
    c_dec = jnp.exp(CHUNK * log_g)
    qc = q.reshape(B, N, CHUNK, H, dk)
    kc = k.reshape(B, N, CHUNK, H, dk)
    vc = v.reshape(B, N, CHUNK, H, dv)
    inner = jnp.einsum('bnihd,bnjhd->bnhij', qc, kc) * dmat.astype(q.dtype)
    inner = jnp.einsum('bnhij,bnjhe->bnihe', inner, vc)

    def step(S, xs):
        qn, kn, vn = xs
        cross = jnp.einsum('bihd,bhde->bihe', qn, S) * q_dec[None, :, :, None]
        S = S * c_dec[None, :, None, None] + jnp.einsum(
            'bjhd,bjhe->bhde', kn * k_dec[None, :, :, None], vn)
        return S, cross

    S0 = jnp.zeros((B, H, dk, dv), jnp.float32)
    _, cross = lax.scan(step, S0, (jnp.moveaxis(qc, 1, 0), jnp.moveaxis(kc, 1, 0),
                                   jnp.moveaxis(vc, 1, 0)))
    out = inner + jnp.moveaxis(cross, 0, 1).astype(inner.dtype)
    return out.reshape(B, Lp, H, dv)


def swa_sinks(q, k, v, sinks):
    B, Lp, H, dh = q.shape
    G = k.shape[2]
    R = H // G
    N = Lp // CHUNK
    W = C_WIN_CHUNKS

    def band(t):
        tc = t.reshape(B, N, CHUNK, G, dh)
        tp = jnp.pad(tc, ((0, 0), (W, 0), (0, 0), (0, 0), (0, 0)))
        bnd = jnp.concatenate([tp[:, j:j + N] for j in range(W + 1)], axis=2)
        meta = jnp.broadcast_to(t[:, META_PAD:CHUNK][:, None], (B, N, N_META, G, dh))
        return jnp.concatenate([meta, bnd], axis=2)

    kb, vb = band(k), band(v)
    key_chunk = jnp.arange(N)[:, None] - W + (jnp.arange((W + 1) * CHUNK) // CHUNK)[None]
    valid = jnp.concatenate([jnp.ones((N, N_META), bool), key_chunk >= 1], axis=1)
    qg = q.reshape(B, N, CHUNK, G, R, dh)
    s = jnp.einsum('bncgrd,bnkgd->bngrck', qg, kb).astype(jnp.float32) * dh ** -0.5
    s = jnp.where(valid[None, :, None, None, None, :], s, NEG)
    sink = jnp.broadcast_to(sinks.reshape(G, R).astype(jnp.float32)[None, None, :, :, None, None],
                            s.shape[:-1] + (1,))
    p = jax.nn.softmax(jnp.concatenate([s, sink], axis=-1), axis=-1)[..., :-1].astype(v.dtype)
    out = jnp.einsum('bngrck,bnkgd->bncgrd', p, vb)
    return out.reshape(B, Lp, H, dh)


def pool_mixer(x, d_mix, d_scale):
    B, L, _ = x.shape
    xg = x.astype(jnp.float32).reshape(B, L, len(POOL_WINDOWS), D_GROUP)
    cs = jnp.pad(jnp.cumsum(xg, axis=1), ((0, 0), (1, 0), (0, 0), (0, 0)))
    t = jnp.arange(L)
    pooled = []
    for gi, w in enumerate(POOL_WINDOWS):
        c = cs[:, :, gi]
        lag = jnp.pad(c, ((0, 0), (w, 0), (0, 0)))[:, 1:L + 1]
        cnt = jnp.minimum(t + 1, w).astype(jnp.float32)[None, :, None]
        pooled.append((c[:, 1:] - lag) / cnt)
    pooled = jnp.stack(pooled, axis=2)
    y = (pooled - xg).astype(x.dtype)
    y = jnp.einsum('blgc,gce->blge', y, d_mix).reshape(B, L, D_WIDTH)
    return y * d_scale


def even_mixer(h, w_in, a_qn, a_kn, w_out, top_k):
    B, L, _ = h.shape
    aq, ak, av, iq, ik, iw, bq, bk, bv, bg = split_cols(h @ w_in, EVEN_SIZES)
    hd = lambda t, n: t.reshape(B, L, n, -1)
    aq = rms_norm(hd(aq, A_HEADS), a_qn)
    ak = rms_norm(hd(ak, A_HEADS), a_kn)
    ya = dsa_attention(aq, ak, hd(av, A_HEADS), hd(iq, IDX_HEADS), ik, iw, top_k)
    ya = ya.reshape(B, L, -1)
    yb = retention(pad_left(hd(bq, B_HEADS)), pad_left(hd(bk, B_HEADS)),
                   pad_left(hd(bv, B_HEADS)))[:, META_PAD:]
    yb = rms_norm(yb).astype(h.dtype).reshape(B, L, -1) * jax.nn.silu(bg)
    return jnp.concatenate([ya, yb], axis=-1) @ w_out


def odd_mixer(h, w_in, c_qn, c_kn, c_sinks, d_mix, d_scale, w_out):
    B, L, _ = h.shape
    cq, ck, cv, dx = split_cols(h @ w_in, ODD_SIZES)
    hd = lambda t, n: t.reshape(B, L, n, -1)
    cq = rms_norm(hd(cq, C_HEADS), c_qn)
    ck = rms_norm(hd(ck, C_KV_HEADS), c_kn)
    yc = swa_sinks(pad_left(cq), pad_left(ck), pad_left(hd(cv, C_KV_HEADS)), c_sinks)
    yc = yc[:, META_PAD:].reshape(B, L, -1)
    yd = pool_mixer(dx, d_mix, d_scale)
    return jnp.concatenate([yc, yd], axis=-1) @ w_out


def setup_inputs(seed: int = 0) -> dict:
    key = jax.random.key(seed)
    ks = jax.random.split(key, 21)
    f32 = jnp.float32
    nrm = lambda k, shape, fan: jax.random.normal(k, shape, f32) * fan ** -0.5
    gain = lambda k, shape: 1.0 + 0.05 * jax.random.normal(k, shape, f32)
    return {
        "x": jax.random.normal(ks[0], (BATCH, SEQ, D_MODEL), f32),
        "meta_tokens": jax.random.normal(ks[1], (N_META, D_MODEL), f32),
        "ffn1_norm": gain(ks[2], (DEPTH, D_MODEL)),
        "ffn1_w_in": nrm(ks[3], (DEPTH, D_MODEL, 2 * D_FF), D_MODEL),
        "ffn1_w_out": nrm(ks[4], (DEPTH, D_FF, D_MODEL), D_FF),
        "mix_norm": gain(ks[5], (DEPTH, D_MODEL)),
        "ffn2_norm": gain(ks[6], (DEPTH, D_MODEL)),
        "ffn2_w_in": nrm(ks[7], (DEPTH, D_MODEL, 2 * D_FF), D_MODEL),
        "ffn2_w_out": nrm(ks[8], (DEPTH, D_FF, D_MODEL), D_FF),
        "ev_w_in": nrm(ks[9], (N_EVEN, D_MODEL, EVEN_IN), D_MODEL),
        "ev_a_q_norm": gain(ks[10], (N_EVEN, HEAD_DIM)),
        "ev_a_k_norm": gain(ks[11], (N_EVEN, HEAD_DIM)),
        "ev_w_out": nrm(ks[12], (N_EVEN, EVEN_OUT, D_MODEL), EVEN_OUT),
        "od_w_in": nrm(ks[13], (N_ODD, D_MODEL, ODD_IN), D_MODEL),
        "od_c_q_norm": gain(ks[14], (N_ODD, HEAD_DIM)),
        "od_c_k_norm": gain(ks[15], (N_ODD, HEAD_DIM)),
        "od_c_sinks": 0.5 * jax.random.normal(ks[16], (N_ODD, C_HEADS), f32),
        "od_d_mix": nrm(ks[17], (N_ODD, len(POOL_WINDOWS), D_GROUP, D_GROUP), D_GROUP),
        "od_d_scale": gain(ks[18], (N_ODD, D_WIDTH)),
        "od_w_out": nrm(ks[19], (N_ODD, ODD_OUT, D_MODEL), ODD_OUT),
    }


def reference(x, meta_tokens, ffn1_norm, ffn1_w_in, ffn1_w_out, mix_norm, ffn2_norm,
              ffn2_w_in, ffn2_w_out, ev_w_in, ev_a_q_norm, ev_a_k_norm, ev_w_out,
              od_w_in, od_c_q_norm, od_c_k_norm, od_c_sinks, od_d_mix, od_d_scale,
              od_w_out):
    B, S, D = x.shape
    top_k = min(DSA_MAX_K, S // 4)
    h = jnp.concatenate([jnp.broadcast_to(meta_tokens.astype(x.dtype)[None], (B, N_META, D)), x],
                        axis=1)
    for layer in range(DEPTH):
        h = h + (0.5 * swiglu(rms_norm(h, ffn1_norm[layer]), ffn1_w_in[layer],
                              ffn1_w_out[layer])).astype(h.dtype)
        hn = rms_norm(h, mix_norm[layer])
        if layer % 2 == 0:
            e = layer // 2
            y = even_mixer(hn, ev_w_in[e], ev_a_q_norm[e], ev_a_k_norm[e], ev_w_out[e], top_k)
        else:
            o = layer // 2
            y = odd_mixer(hn, od_w_in[o], od_c_q_norm[o], od_c_k_norm[o], od_c_sinks[o],
                          od_d_mix[o], od_d_scale[o], od_w_out[o])
        h = h + y.astype(h.dtype)
        h = h + (0.5 * swiglu(rms_norm(h, ffn2_norm[layer]), ffn2_w_in[layer],
                              ffn2_w_out[layer])).astype(h.dtype)
    return h[:, N_META:]
```

```python
import functools

import jax
import jax.numpy as jnp
from jax import lax
from jax.experimental import pallas as pl
from jax.experimental.pallas import tpu as pltpu

F32 = jnp.float32
BF16 = jnp.bfloat16

D_MODEL = 1024
D_FF = 2816
CHUNK = 64
N_META = 16
PAD = CHUNK - N_META
HEAD_DIM = 64
A_HEADS = 8
IDX_HEADS = 8
IDX_DIM = 64
TOP_K = 256
B_HEADS = 8
B_VDIM = 128
C_HEADS = 8
C_KV_HEADS = 2
C_WIN_CHUNKS = 2
D_WIDTH = 512
POOL_WINDOWS = (2, 4, 8, 16)
D_GROUP = 128
EPS = 1e-6
NEG = -1e30
BIG = 3e38

VMEM_LIMIT = 56 * 1024 * 1024
ROW_TILE = 512
FF_CHUNK = 256
SEQ_TILE = 192
BISECT_ITERS = 20

_NT = (((1,), (1,)), ((), ()))
_TN = (((0,), (0,)), ((), ()))


def _params(sem):
    return pltpu.CompilerParams(dimension_semantics=sem, vmem_limit_bytes=VMEM_LIMIT)


def _rms(x, gain=None):
    y = x * lax.rsqrt(jnp.mean(x * x, axis=-1, keepdims=True) + EPS)
    return y if gain is None else y * gain


def _resident(shape):
    return pl.BlockSpec(shape, lambda *_: (0,) * len(shape), pipeline_mode=pl.Buffered(1))


def _ffn_kernel(h_ref, g_ref, win_ref, wout_ref, o_ref):
    x = h_ref[...]
    xn = _rms(x, g_ref[...]).astype(BF16)
    acc = jnp.zeros(x.shape, F32)
    for c in range(D_FF // FF_CHUNK):
        lo = c * FF_CHUNK
        g = jnp.dot(xn, win_ref[:, lo:lo + FF_CHUNK], preferred_element_type=F32)
        u = jnp.dot(xn, win_ref[:, D_FF + lo:D_FF + lo + FF_CHUNK], preferred_element_type=F32)
        a = (jax.nn.silu(g) * u).astype(BF16)
        acc = acc + jnp.dot(a, wout_ref[lo:lo + FF_CHUNK, :], preferred_element_type=F32)
    o_ref[...] = x + 0.5 * acc


def _ffn(h, gain, w_in, w_out):
    rows = h.shape[0]
    return pl.pallas_call(
        _ffn_kernel,
        grid=(rows // ROW_TILE,),
        in_specs=[
            pl.BlockSpec((ROW_TILE, D_MODEL), lambda i: (i, 0)),
            _resident((1, D_MODEL)),
            _resident((D_MODEL, 2 * D_FF)),
            _resident((D_FF, D_MODEL)),
        ],
        out_specs=pl.BlockSpec((ROW_TILE, D_MODEL), lambda i: (i, 0)),
        out_shape=jax.ShapeDtypeStruct((rows, D_MODEL), F32),
        compiler_params=_params(("parallel",)),
        name="ffn",
    )(h, gain.reshape(1, D_MODEL), w_in, w_out)


def _proj_kernel(h_ref, g_ref, w_ref, *o_refs, splits):
    xn = _rms(h_ref[...], g_ref[...]).astype(BF16)
    off = 0
    for o_ref, n in zip(o_refs, splits):
        for c0 in range(0, n, 512):
            cw = min(512, n - c0)
            o_ref[:, c0:c0 + cw] = jnp.dot(xn, w_ref[:, off + c0:off + c0 + cw],
                                           preferred_element_type=F32)
        off += n


def _proj(h, gain, w, splits):
    rows = h.shape[0]
    n_total = sum(splits)
    tile = ROW_TILE // 2
    return pl.pallas_call(
        functools.partial(_proj_kernel, splits=splits),
        grid=(rows // tile,),
        in_specs=[
            pl.BlockSpec((tile, D_MODEL), lambda i: (i, 0)),
            _resident((1, D_MODEL)),
            _resident((D_MODEL, n_total)),
        ],
        out_specs=[pl.BlockSpec((tile, n), lambda i: (i, 0)) for n in splits],
        out_shape=[jax.ShapeDtypeStruct((rows, n), F32) for n in splits],
        compiler_params=_params(("parallel",)),
        name="mixer_in_proj",
    )(h, gain.reshape(1, D_MODEL), w)


def _out_proj_kernel(h_ref, ya_ref, yb_ref, w_ref, o_ref):
    na = ya_ref.shape[1]
    y = jnp.dot(ya_ref[...].astype(BF16), w_ref[0:na, :], preferred_element_type=F32)
    y = y + jnp.dot(yb_ref[...].astype(BF16), w_ref[na:, :], preferred_element_type=F32)
    o_ref[...] = h_ref[...] + y


def _out_proj(h, ya, yb, w):
    rows = h.shape[0]
    na, nb = ya.shape[1], yb.shape[1]
    return pl.pallas_call(
        _out_proj_kernel,
        grid=(rows // ROW_TILE,),
        in_specs=[
            pl.BlockSpec((ROW_TILE, D_MODEL), lambda i: (i, 0)),
            pl.BlockSpec((ROW_TILE, na), lambda i: (i, 0)),
            pl.BlockSpec((ROW_TILE, nb), lambda i: (i, 0)),
            _resident((na + nb, D_MODEL)),
        ],
        out_specs=pl.BlockSpec((ROW_TILE, D_MODEL), lambda i: (i, 0)),
        out_shape=jax.ShapeDtypeStruct((rows, D_MODEL), F32),
        compiler_params=_params(("parallel",)),
        name="mixer_out_proj",
    )(h, ya, yb, w)


def _row_count(mask):
    return jnp.sum(jnp.where(mask, 1.0, 0.0), axis=-1, keepdims=True)


def _any(mask):
    return jnp.max(jnp.where(mask, 1.0, 0.0)) > 0.0


def _topk_select(sm, admissible, search_row, col):
    mn = jnp.min(jnp.where(admissible, sm, BIG), axis=-1, keepdims=True)
    hi = jnp.max(sm, axis=-1, keepdims=True)
    lo = mn - (jnp.abs(mn) + 1.0)

    def step(lo, hi, mid):
        up = _row_count(sm > mid) >= TOP_K
        return jnp.where(up, mid, lo), jnp.where(up, hi, mid)

    def coarse(_, c):
        lo, hi = c
        return step(lo, hi, 0.5 * (lo + hi))

    lo, hi = lax.fori_loop(0, BISECT_ITERS, coarse, (lo, hi))

    def bounds(lo, hi):
        t_lo = jnp.min(jnp.where(sm > lo, sm, BIG), axis=-1, keepdims=True)
        t_hi = jnp.max(jnp.where(sm <= hi, sm, -BIG), axis=-1, keepdims=True)
        return t_lo, t_hi

    def unfinished(c):
        _, _, t_lo, t_hi = c
        return _any((t_lo != t_hi) & search_row)

    def refine(c):
        lo, hi, t_lo, t_hi = c
        mid = 0.5 * (t_lo + t_hi)
        mid = jnp.where(mid >= t_hi, t_lo, mid)
        lo, hi = step(lo, hi, mid)
        return (lo, hi) + bounds(lo, hi)

    _, _, _, thr = lax.while_loop(unfinished, refine, (lo, hi) + bounds(lo, hi))

    above = sm > thr
    tied = sm == thr
    need = TOP_K - _row_count(above)

    def last_tie_column():
        def tie_step(_, c):
            lo_i, hi_i = c
            mid_i = (lo_i + hi_i) >> 1
            ok = _row_count(tied & (col <= mid_i)) >= need
            return jnp.where(ok, lo_i, mid_i), jnp.where(ok, mid_i, hi_i)

        n = sm.shape[1]
        init = (jnp.full(thr.shape, -1, jnp.int32), jnp.full(thr.shape, n - 1, jnp.int32))
        return lax.fori_loop(0, n.bit_length(), tie_step, init)[1]

    overflow = _any((_row_count(tied) > need) & search_row)
    limit = lax.cond(overflow, last_tie_column,
                     lambda: jnp.full(thr.shape, sm.shape[1], jnp.int32))
    return above | (tied & (col <= limit))


def _dsa_kernel(aq_ref, ak_ref, av_ref, iq_ref, ikw_q_ref, ikw_k_ref, qn_ref, kn_ref, o_ref,
                k_s, v_s, ik_s, bias_s):
    i = pl.program_id(1)
    tq, lp = bias_s.shape

    @pl.when(i == 0)
    def _():
        for h in range(A_HEADS):
            sl = slice(h * HEAD_DIM, (h + 1) * HEAD_DIM)
            k_s[h] = _rms(ak_ref[0, :, sl], kn_ref[...]).astype(BF16)
            v_s[h] = av_ref[0, :, sl].astype(BF16)
        ik_s[...] = ikw_k_ref[0, :, 0:IDX_DIM].astype(BF16)

    row = i * tq + lax.broadcasted_iota(jnp.int32, (tq, 1), 0)
    col = lax.broadcasted_iota(jnp.int32, (1, lp), 1)
    q_chunk = row // CHUNK
    admissible = (col // CHUNK <= q_chunk) & (col >= PAD)

    iw = ikw_q_ref[0, :, IDX_DIM:IDX_DIM + IDX_HEADS] * (IDX_HEADS ** -0.5 * IDX_DIM ** -0.5)
    score = jnp.zeros((tq, lp), F32)
    for h in range(IDX_HEADS):
        iq = iq_ref[0, :, h * IDX_DIM:(h + 1) * IDX_DIM].astype(BF16)
        logits = lax.dot_general(iq, ik_s[...], _NT, preferred_element_type=F32)
        score = score + iw[:, h:h + 1] * jnp.maximum(logits, 0.0)
    sm = jnp.where(admissible, score, NEG)

    search_row = N_META + CHUNK * q_chunk >= TOP_K
    first_search_block = (TOP_K - N_META + CHUNK - 1) // CHUNK * CHUNK // tq

    @pl.when(i < first_search_block)
    def _():
        bias_s[...] = jnp.where(admissible, 0.0, NEG)

    @pl.when(i >= first_search_block)
    def _():
        keep = admissible & (_topk_select(sm, admissible, search_row, col)
                             | jnp.logical_not(search_row))
        bias_s[...] = jnp.where(keep, 0.0, NEG)

    bias = bias_s[...]
    outs = []
    for h in range(A_HEADS):
        q = _rms(aq_ref[0, :, h * HEAD_DIM:(h + 1) * HEAD_DIM], qn_ref[...]).astype(BF16)
        s = lax.dot_general(q, k_s[h], _NT, preferred_element_type=F32) * HEAD_DIM ** -0.5 + bias
        p = jnp.exp(s - jnp.max(s, axis=-1, keepdims=True))
        denom = jnp.sum(p, axis=-1, keepdims=True)
        outs.append(jnp.dot(p.astype(BF16), v_s[h], preferred_element_type=F32) / denom)
    out = jnp.concatenate(outs, axis=-1)
    o_ref[0] = jnp.where(row >= PAD, out, 0.0)


def _dsa(aq, ak, av, iq, ikw, q_gain, k_gain):
    b, lp, _ = aq.shape
    width = A_HEADS * HEAD_DIM
    tq = SEQ_TILE
    q_spec = lambda n: pl.BlockSpec((1, tq, n), lambda bi, i: (bi, i, 0))
    full_spec = lambda n: pl.BlockSpec((1, lp, n), lambda bi, i: (bi, 0, 0))
    return pl.pallas_call(
        _dsa_kernel,
        grid=(b, lp // tq),
        in_specs=[q_spec(width), full_spec(width), full_spec(width), q_spec(width),
                  q_spec(128), full_spec(128), _resident((1, HEAD_DIM)), _resident((1, HEAD_DIM))],
        out_specs=q_spec(width),
        out_shape=jax.ShapeDtypeStruct((b, lp, width), F32),
        scratch_shapes=[
            pltpu.VMEM((A_HEADS, lp, HEAD_DIM), BF16),
            pltpu.VMEM((A_HEADS, lp, HEAD_DIM), BF16),
            pltpu.VMEM((lp, IDX_DIM), BF16),
            pltpu.VMEM((tq, lp), F32),
        ],
        compiler_params=_params(("parallel", "arbitrary")),
        name="dsa_attention",
    )(aq, ak, av, iq, ikw, ikw, q_gain.reshape(1, HEAD_DIM), k_gain.reshape(1, HEAD_DIM))


def _rotate_half(x, first_half):
    n = x.shape[-1]
    half = HEAD_DIM // 2
    return jnp.where(first_half, -pltpu.roll(x, n - half, 1), pltpu.roll(x, half, 1))


def _retention_kernel(q_ref, k_ref, v_ref, g_ref, cos_ref, sin_ref, dmat_ref, qdec_ref, kdec_ref,
                      cdec_ref, o_ref, state):
    i = pl.program_id(1)

    @pl.when(i == 0)
    def _():
        state[...] = jnp.zeros(state.shape, F32)

    cos, sin = cos_ref[...], sin_ref[...]
    lane = lax.broadcasted_iota(jnp.int32, (1, q_ref.shape[2]), 1)
    first_half = lane % HEAD_DIM < HEAD_DIM // 2
    q = q_ref[0]
    k = k_ref[0]
    q = q * cos + _rotate_half(q, first_half) * sin
    k = (k * cos + _rotate_half(k, first_half) * sin) * HEAD_DIM ** -0.5
    for h in range(B_HEADS):
        qh = q[:, h * HEAD_DIM:(h + 1) * HEAD_DIM].astype(BF16)
        kh = k[:, h * HEAD_DIM:(h + 1) * HEAD_DIM]
        vsl = slice(h * B_VDIM, (h + 1) * B_VDIM)
        vh = v_ref[0, :, vsl].astype(BF16)
        s_h = state[h]
        inner = lax.dot_general(qh, kh.astype(BF16), _NT, preferred_element_type=F32) * dmat_ref[h]
        y = jnp.dot(inner.astype(BF16), vh, preferred_element_type=F32)
        y = y + jnp.dot(qh, s_h.astype(BF16), preferred_element_type=F32) * qdec_ref[h]
        state[h] = s_h * cdec_ref[h] + lax.dot_general(
            (kh * kdec_ref[h]).astype(BF16), vh, _TN, preferred_element_type=F32)
        o_ref[0, :, vsl] = _rms(y) * jax.nn.silu(g_ref[0, :, vsl])


def _retention(q, k, v, gate):
    b, lp, _ = q.shape
    t = SEQ_TILE
    pos = (jnp.arange(lp) - PAD).astype(F32)
    inv = 1.0 / (10000.0 ** (jnp.arange(0, HEAD_DIM, 2, dtype=F32) / HEAD_DIM))
    ang = pos[:, None] * inv[None]
    cos = jnp.tile(jnp.cos(ang), (1, 2 * B_HEADS))
    sin = jnp.tile(jnp.sin(ang), (1, 2 * B_HEADS))
    log_g = jnp.log(1.0 - 2.0 ** (-5.0 - jnp.arange(B_HEADS, dtype=F32)))
    idx = jnp.arange(t, dtype=F32)
    rel = idx[:, None] - idx[None, :]
    dmat = jnp.where(rel >= 0, jnp.exp(jnp.maximum(rel, 0.0)[None] * log_g[:, None, None]), 0.0)
    qdec = jnp.exp((idx + 1.0)[None, :, None] * log_g[:, None, None])
    kdec = jnp.exp((t - 1.0 - idx)[None, :, None] * log_g[:, None, None])
    cdec = jnp.exp(t * log_g)[:, None, None]

    qk_w, v_w = B_HEADS * HEAD_DIM, B_HEADS * B_VDIM
    seq = lambda n: pl.BlockSpec((1, t, n), lambda bi, i: (bi, i, 0))
    return pl.pallas_call(
        _retention_kernel,
        grid=(b, lp // t),
        in_specs=[seq(qk_w), seq(qk_w), seq(v_w), seq(v_w),
                  pl.BlockSpec((t, qk_w), lambda bi, i: (i, 0)),
                  pl.BlockSpec((t, qk_w), lambda bi, i: (i, 0)),
                  _resident((B_HEADS, t, t)), _resident((B_HEADS, t, 1)),
                  _resident((B_HEADS, t, 1)), _resident((B_HEADS, 1, 1))],
        out_specs=seq(v_w),
        out_shape=jax.ShapeDtypeStruct((b, lp, v_w), F32),
        scratch_shapes=[pltpu.VMEM((B_HEADS, HEAD_DIM, B_VDIM), F32)],
        compiler_params=_params(("parallel", "arbitrary")),
        name="retention",
    )(q, k, v, gate, cos, sin, dmat, qdec, kdec, cdec)


def _swa_kernel(q_ref, kv_ref, qn_ref, kn_ref, sink_ref, o_ref, k_s, v_s):
    i = pl.program_id(1)
    tq = q_ref.shape[1]
    chunks = tq // CHUNK
    win = (chunks + C_WIN_CHUNKS) * CHUNK
    kv_w = C_KV_HEADS * HEAD_DIM

    @pl.when(i == 0)
    def _():
        for g in range(C_KV_HEADS):
            k_s[g] = _rms(kv_ref[0, :, g * HEAD_DIM:(g + 1) * HEAD_DIM], kn_ref[...]).astype(BF16)
            v_s[g] = kv_ref[0, :, kv_w + g * HEAD_DIM:kv_w + (g + 1) * HEAD_DIM].astype(BF16)

    first_chunk = jnp.maximum(i * chunks - C_WIN_CHUNKS, 0)
    start = pl.multiple_of(first_chunk * CHUNK, CHUNK)
    row = i * tq + lax.broadcasted_iota(jnp.int32, (tq, 1), 0)
    q_chunk = row // CHUNK
    k_chunk = first_chunk + lax.broadcasted_iota(jnp.int32, (1, win), 1) // CHUNK
    in_window = (k_chunk >= 1) & (k_chunk <= q_chunk) & (k_chunk >= q_chunk - C_WIN_CHUNKS)
    is_meta = lax.broadcasted_iota(jnp.int32, (1, CHUNK), 1) >= PAD

    outs = []
    for h in range(C_HEADS):
        g = h // (C_HEADS // C_KV_HEADS)
        q = _rms(q_ref[0, :, h * HEAD_DIM:(h + 1) * HEAD_DIM], qn_ref[...]).astype(BF16)
        s_w = lax.dot_general(q, k_s[g, pl.ds(start, win), :], _NT,
                              preferred_element_type=F32) * HEAD_DIM ** -0.5
        s_m = lax.dot_general(q, k_s[g, 0:CHUNK, :], _NT,
                              preferred_element_type=F32) * HEAD_DIM ** -0.5
        s_w = jnp.where(in_window, s_w, NEG)
        s_m = jnp.where(is_meta, s_m, NEG)
        sink = sink_ref[0:1, h:h + 1]
        m = jnp.maximum(jnp.maximum(jnp.max(s_w, axis=-1, keepdims=True),
                                    jnp.max(s_m, axis=-1, keepdims=True)), sink)
        p_w = jnp.exp(s_w - m)
        p_m = jnp.exp(s_m - m)
        denom = (jnp.sum(p_w, axis=-1, keepdims=True) + jnp.sum(p_m, axis=-1, keepdims=True)
                 + jnp.exp(sink - m))
        y = jnp.dot(p_w.astype(BF16), v_s[g, pl.ds(start, win), :], preferred_element_type=F32)
        y = y + jnp.dot(p_m.astype(BF16), v_s[g, 0:CHUNK, :], preferred_element_type=F32)
        outs.append(y / denom)
    out = jnp.concatenate(outs, axis=-1)
    o_ref[0] = jnp.where(row >= PAD, out, 0.0)


def _swa(q, kv, q_gain, k_gain, sinks):
    b, lp, width = q.shape
    tq = SEQ_TILE
    return pl.pallas_call(
        _swa_kernel,
        grid=(b, lp // tq),
        in_specs=[pl.BlockSpec((1, tq, width), lambda bi, i: (bi, i, 0)),
                  pl.BlockSpec((1, lp, kv.shape[2]), lambda bi, i: (bi, 0, 0)),
                  _resident((1, HEAD_DIM)), _resident((1, HEAD_DIM)), _resident((1, C_HEADS))],
        out_specs=pl.BlockSpec((1, tq, width), lambda bi, i: (bi, i, 0)),
        out_shape=jax.ShapeDtypeStruct((b, lp, width), F32),
        scratch_shapes=[pltpu.VMEM((C_KV_HEADS, lp, HEAD_DIM), BF16),
                        pltpu.VMEM((C_KV_HEADS, lp, HEAD_DIM), BF16)],
        compiler_params=_params(("parallel", "arbitrary")),
        name="swa_sinks",
    )(q, kv, q_gain.reshape(1, HEAD_DIM), k_gain.reshape(1, HEAD_DIM), sinks.reshape(1, C_HEADS))


def _pool_kernel(x_ref, mix_ref, scale_ref, o_ref):
    lp = x_ref.shape[1]
    row = lax.broadcasted_iota(jnp.int32, (lp, 1), 0)
    t = row - PAD
    outs = []
    for gi, w in enumerate(POOL_WINDOWS):
        x = x_ref[0, :, gi * D_GROUP:(gi + 1) * D_GROUP]
        s, shift = x, 1
        while shift < w:
            s = s + jnp.where(row >= shift, pltpu.roll(s, shift, 0), 0.0)
            shift *= 2
        count = jnp.maximum(jnp.minimum(t + 1, w), 1).astype(F32)
        y = (s / count - x).astype(BF16)
        outs.append(jnp.dot(y, mix_ref[gi].astype(BF16), preferred_element_type=F32))
    out = jnp.concatenate(outs, axis=-1) * scale_ref[...]
    o_ref[0] = jnp.where(row >= PAD, out, 0.0)


def _pool(x, mix, scale):
    b, lp, width = x.shape
    n_groups = len(POOL_WINDOWS)
    return pl.pallas_call(
        _pool_kernel,
        grid=(b,),
        in_specs=[pl.BlockSpec((1, lp, width), lambda bi: (bi, 0, 0)),
                  _resident((n_groups, D_GROUP, D_GROUP)), _resident((1, width))],
        out_specs=pl.BlockSpec((1, lp, width), lambda bi: (bi, 0, 0)),
        out_shape=jax.ShapeDtypeStruct((b, lp, width), F32),
        compiler_params=_params(("parallel",)),
        name="pool_mixer",
    )(x, mix, scale.reshape(1, width))


def kernel(x, meta_tokens, ffn1_norm, ffn1_w_in, ffn1_w_out, mix_norm, ffn2_norm, ffn2_w_in,
           ffn2_w_out, ev_w_in, ev_a_q_norm, ev_a_k_norm, ev_w_out, od_w_in, od_c_q_norm,
           od_c_k_norm, od_c_sinks, od_d_mix, od_d_scale, od_w_out):
    b, s, d = x.shape
    lp = PAD + N_META + s
    assert d == D_MODEL and lp % SEQ_TILE == 0 and (b * lp) % ROW_TILE == 0
    assert min(TOP_K, s // 4) == TOP_K
    depth = ffn1_norm.shape[0]

    h = jnp.concatenate([jnp.zeros((b, PAD, d), x.dtype),
                         jnp.broadcast_to(meta_tokens.astype(x.dtype)[None], (b, N_META, d)), x],
                        axis=1).reshape(b * lp, d)
    seq = lambda a: a.reshape(b, lp, a.shape[-1])
    flat = lambda a: a.reshape(b * lp, a.shape[-1])

    for layer in range(depth):
        h = _ffn(h, ffn1_norm[layer], ffn1_w_in[layer].astype(BF16), ffn1_w_out[layer].astype(BF16))
        if layer % 2 == 0:
            e = layer // 2
            w = ev_w_in[e]
            a_w = 3 * A_HEADS * HEAD_DIM + IDX_HEADS * IDX_DIM
            i_w = IDX_DIM + IDX_HEADS
            w = jnp.concatenate([w[:, :a_w], jnp.pad(w[:, a_w:a_w + i_w], ((0, 0), (0, 128 - i_w))),
                                 w[:, a_w + i_w:]], axis=1).astype(BF16)
            aq, ak, av, iq, ikw, bq, bk, bv, bg = _proj(
                h, mix_norm[layer], w, (512, 512, 512, 512, 128, 512, 512, 1024, 1024))
            ya = _dsa(seq(aq), seq(ak), seq(av), seq(iq), seq(ikw), ev_a_q_norm[e], ev_a_k_norm[e])
            yb = _retention(seq(bq), seq(bk), seq(bv), seq(bg))
            h = _out_proj(h, flat(ya), flat(yb), ev_w_out[e].astype(BF16))
        else:
            o = layer // 2
            cq, ckv, dx = _proj(h, mix_norm[layer], od_w_in[o].astype(BF16), (512, 256, 512))
            yc = _swa(seq(cq), seq(ckv), od_c_q_norm[o], od_c_k_norm[o], od_c_sinks[o])
            yd = _pool(seq(dx), od_d_mix[o], od_d_scale[o])
            h = _out_proj(h, flat(yc), flat(yd), od_w_out[o].astype(BF16))
        h = _ffn(h, ffn2_norm[layer], ffn2_w_in[layer].astype(BF16), ffn2_w_out[layer].astype(BF16))
    return h.reshape(b, lp, d)[:, PAD + N_META:]
```

```python
import functools

import jax
import jax.numpy as jnp
from jax import lax
from jax.experimental import pallas as pl
from jax.experimental.pallas import tpu as pltpu

F32 = jnp.float32
BF16 = jnp.bfloat16

D_MODEL = 1024
D_FF = 2816
CHUNK = 64
N_META = 16
PAD = CHUNK - N_META
HEAD_DIM = 64
LANES = 128
A_HEADS = 8
IDX_HEADS = 8
IDX_DIM = 64
TOP_K = 256
B_HEADS = 8
B_VDIM = 128
C_HEADS = 8
C_KV_HEADS = 2
C_WIN_CHUNKS = 2
D_WIDTH = 512
POOL_WINDOWS = (2, 4, 8, 16)
D_GROUP = 128
EPS = 1e-6
NEG = -1e30
BIG = 3e38
LOG2E = 1.4426950408889634

VMEM_LIMIT = 56 * 1024 * 1024
ROW_TILE = 512
FF_CHUNK = 256
PROJ_TILE = 704
SEQ_TILE = 192
KEY_STEP = 384
RET_TILE = 352
BISECT_ITERS = 20

_NT = (((1,), (1,)), ((), ()))
_TN = (((0,), (0,)), ((), ()))


def _params(sem):
    return pltpu.CompilerParams(dimension_semantics=sem, vmem_limit_bytes=VMEM_LIMIT)


def _rms(x, gain=None):
    y = x * lax.rsqrt(jnp.mean(x * x, axis=-1, keepdims=True) + EPS)
    return y if gain is None else y * gain


def _resident(shape):
    return pl.BlockSpec(shape, lambda *_: (0,) * len(shape), pipeline_mode=pl.Buffered(1))


def _low_half():
    return lax.broadcasted_iota(jnp.int32, (1, LANES), 1) < HEAD_DIM


def _keep_head(x, half, low_half):
    return jnp.where(low_half if half == 0 else jnp.logical_not(low_half), x, jnp.zeros_like(x))


def _ffn_kernel(h_ref, g_ref, win_ref, wout_ref, o_ref):
    x = h_ref[...]
    xn = _rms(x, g_ref[...]).astype(BF16)
    acc = jnp.zeros(x.shape, F32)
    for c in range(D_FF // FF_CHUNK):
        lo = c * FF_CHUNK
        g = jnp.dot(xn, win_ref[:, lo:lo + FF_CHUNK], preferred_element_type=F32)
        u = jnp.dot(xn, win_ref[:, D_FF + lo:D_FF + lo + FF_CHUNK], preferred_element_type=F32)
        a = (jax.nn.silu(g) * u).astype(BF16)
        acc = acc + jnp.dot(a, wout_ref[lo:lo + FF_CHUNK, :], preferred_element_type=F32)
    o_ref[...] = x + 0.5 * acc


def _ffn(h, gain, w_in, w_out):
    rows = h.shape[0]
    return pl.pallas_call(
        _ffn_kernel,
        grid=(rows // ROW_TILE,),
        in_specs=[
            pl.BlockSpec((ROW_TILE, D_MODEL), lambda i: (i, 0)),
            _resident((1, D_MODEL)),
            _resident((D_MODEL, 2 * D_FF)),
            _resident((D_FF, D_MODEL)),
        ],
        out_specs=pl.BlockSpec((ROW_TILE, D_MODEL), lambda i: (i, 0)),
        out_shape=jax.ShapeDtypeStruct((rows, D_MODEL), F32),
        compiler_params=_params(("parallel",)),
        name="ffn",
    )(h, gain.reshape(1, D_MODEL), w_in, w_out)


def _head_norm(y, seg_ref, gain):
    sq = (y * y).astype(BF16)
    ms = jnp.concatenate(
        [jnp.dot(sq[:, p * LANES:(p + 1) * LANES], seg_ref[...], preferred_element_type=F32)
         for p in range(y.shape[1] // LANES)], axis=-1)
    return y * lax.rsqrt(ms + EPS) * gain


def _rotate_half(x, first_half):
    n = x.shape[-1]
    half = HEAD_DIM // 2
    return jnp.where(first_half, -pltpu.roll(x, n - half, 1), pltpu.roll(x, half, 1))


def _even_proj_kernel(h_ref, g_ref, w_ref, seg_ref, qg_ref, kg_ref, cos_ref, sin_ref,
                      aq_ref, ak_ref, av_ref, iq_ref, ik_ref, iw_ref, bq_ref, bk_ref, bv_ref,
                      bg_ref):
    xn = _rms(h_ref[0], g_ref[...]).astype(BF16)
    off = [0]

    def cols(n):
        y = jnp.dot(xn, w_ref[:, off[0]:off[0] + n], preferred_element_type=F32)
        off[0] += n
        return y

    w = A_HEADS * HEAD_DIM
    aq_ref[0] = _head_norm(cols(w), seg_ref, qg_ref[...]).astype(BF16)
    ak_ref[0] = _head_norm(cols(w), seg_ref, kg_ref[...]).astype(BF16)
    av_ref[0] = cols(w).astype(BF16)
    iq_ref[0] = cols(IDX_HEADS * IDX_DIM).astype(BF16)
    ik_ref[0] = cols(LANES).astype(BF16)
    iw_ref[0] = cols(LANES) * (IDX_HEADS ** -0.5 * IDX_DIM ** -0.5)
    cos, sin = cos_ref[...], sin_ref[...]
    first_half = lax.broadcasted_iota(jnp.int32, (1, w), 1) % HEAD_DIM < HEAD_DIM // 2
    q = cols(w)
    bq_ref[0] = (q * cos + _rotate_half(q, first_half) * sin).astype(BF16)
    k = cols(w)
    bk_ref[0] = ((k * cos + _rotate_half(k, first_half) * sin) * HEAD_DIM ** -0.5).astype(BF16)
    for c in range(2):
        sl = slice(c * w, (c + 1) * w)
        bv_ref[0, :, sl] = cols(w).astype(BF16)
    for c in range(2):
        sl = slice(c * w, (c + 1) * w)
        bg_ref[0, :, sl] = cols(w)


def _odd_proj_kernel(h_ref, g_ref, w_ref, seg_ref, qg_ref, kg_ref, cq_ref, ck_ref, cv_ref, dx_ref):
    xn = _rms(h_ref[0], g_ref[...]).astype(BF16)
    off = [0]

    def cols(n):
        y = jnp.dot(xn, w_ref[:, off[0]:off[0] + n], preferred_element_type=F32)
        off[0] += n
        return y

    kv_w = 2 * C_KV_HEADS * HEAD_DIM
    cq_ref[0] = _head_norm(cols(C_HEADS * HEAD_DIM), seg_ref, qg_ref[...]).astype(BF16)
    ck_ref[0] = _head_norm(cols(kv_w), seg_ref, kg_ref[...]).astype(BF16)
    cv_ref[0] = cols(kv_w).astype(BF16)
    dx_ref[0] = cols(D_WIDTH)


def _seg_matrix():
    g = jnp.arange(LANES) // HEAD_DIM
    return jnp.where(g[:, None] == g[None, :], 1.0 / HEAD_DIM, 0.0).astype(BF16)


def _proj_call(body, name, h, gain, w, extra_in, extra_specs, outs):
    b, lp, _ = h.shape
    t = PROJ_TILE
    seq = lambda n: pl.BlockSpec((1, t, n), lambda bi, i: (bi, i, 0))
    return pl.pallas_call(
        body,
        grid=(b, lp // t),
        in_specs=[seq(D_MODEL), _resident((1, D_MODEL)), _resident(w.shape),
                  _resident((LANES, LANES))] + extra_specs,
        out_specs=[seq(n) for n, _ in outs],
        out_shape=[jax.ShapeDtypeStruct((b, lp, n), dt) for n, dt in outs],
        compiler_params=_params(("parallel", "parallel")),
        name=name,
    )(h, gain.reshape(1, D_MODEL), w, _seg_matrix(), *extra_in)


def _rotary_tables(lp):
    pos = (jnp.arange(lp) - PAD).astype(F32)
    inv = 1.0 / (10000.0 ** (jnp.arange(0, HEAD_DIM, 2, dtype=F32) / HEAD_DIM))
    ang = pos[:, None] * inv[None]
    return (jnp.tile(jnp.cos(ang), (1, 2 * B_HEADS)), jnp.tile(jnp.sin(ang), (1, 2 * B_HEADS)))


def _even_proj(h, gain, w, q_gain, k_gain):
    lp = h.shape[1]
    a_w = 3 * A_HEADS * HEAD_DIM + IDX_HEADS * IDX_DIM
    ik = w[:, a_w:a_w + IDX_DIM]
    iw = jnp.pad(w[:, a_w + IDX_DIM:a_w + IDX_DIM + IDX_HEADS], ((0, 0), (0, LANES - IDX_HEADS)))
    w = jnp.concatenate([w[:, :a_w], ik, ik, iw, w[:, a_w + IDX_DIM + IDX_HEADS:]], axis=1)
    cos, sin = _rotary_tables(lp)
    w_heads = A_HEADS * HEAD_DIM
    qg = jnp.tile(q_gain, A_HEADS).reshape(1, w_heads) * (HEAD_DIM ** -0.5 * LOG2E)
    kg = jnp.tile(k_gain, A_HEADS).reshape(1, w_heads)
    table = pl.BlockSpec((PROJ_TILE, w_heads), lambda bi, i: (i, 0))
    outs = [(w_heads, BF16)] * 4 + [(LANES, BF16), (LANES, F32), (w_heads, BF16), (w_heads, BF16),
                                    (B_HEADS * B_VDIM, BF16), (B_HEADS * B_VDIM, F32)]
    return _proj_call(_even_proj_kernel, "even_in_proj", h, gain, w.astype(BF16),
                      [qg, kg, cos, sin],
                      [_resident((1, w_heads)), _resident((1, w_heads)), table, table], outs)


def _odd_proj(h, gain, w, q_gain, k_gain):
    q_w = C_HEADS * HEAD_DIM
    kv_w = C_KV_HEADS * HEAD_DIM
    dup = lambda m: jnp.concatenate(
        [m[:, g * HEAD_DIM:(g + 1) * HEAD_DIM] for g in range(C_KV_HEADS) for _ in range(2)], axis=1)
    w = jnp.concatenate([w[:, :q_w], dup(w[:, q_w:q_w + kv_w]), dup(w[:, q_w + kv_w:q_w + 2 * kv_w]),
                         w[:, q_w + 2 * kv_w:]], axis=1)
    qg = jnp.tile(q_gain, C_HEADS).reshape(1, q_w) * (HEAD_DIM ** -0.5 * LOG2E)
    kg = jnp.tile(k_gain, 2 * C_KV_HEADS).reshape(1, 2 * kv_w)
    outs = [(q_w, BF16), (2 * kv_w, BF16), (2 * kv_w, BF16), (D_WIDTH, F32)]
    return _proj_call(_odd_proj_kernel, "odd_in_proj", h, gain, w.astype(BF16), [qg, kg],
                      [_resident((1, q_w)), _resident((1, 2 * kv_w))], outs)


def _out_proj_kernel(h_ref, ya_ref, yb_ref, w_ref, o_ref):
    na = ya_ref.shape[1]
    y = jnp.dot(ya_ref[...], w_ref[0:na, :], preferred_element_type=F32)
    y = y + jnp.dot(yb_ref[...], w_ref[na:, :], preferred_element_type=F32)
    o_ref[...] = h_ref[...] + y


def _out_proj(h, ya, yb, w):
    rows = h.shape[0]
    na, nb = ya.shape[1], yb.shape[1]
    return pl.pallas_call(
        _out_proj_kernel,
        grid=(rows // ROW_TILE,),
        in_specs=[
            pl.BlockSpec((ROW_TILE, D_MODEL), lambda i: (i, 0)),
            pl.BlockSpec((ROW_TILE, na), lambda i: (i, 0)),
            pl.BlockSpec((ROW_TILE, nb), lambda i: (i, 0)),
            _resident((na + nb, D_MODEL)),
        ],
        out_specs=pl.BlockSpec((ROW_TILE, D_MODEL), lambda i: (i, 0)),
        out_shape=jax.ShapeDtypeStruct((rows, D_MODEL), F32),
        compiler_params=_params(("parallel",)),
        name="mixer_out_proj",
    )(h, ya, yb, w)


def _row_count(mask):
    return jnp.sum(jnp.where(mask, 1.0, 0.0), axis=-1, keepdims=True)


def _any(mask):
    return jnp.max(jnp.where(mask, 1.0, 0.0)) > 0.0


def _topk_select(sm, admissible, search_row, col):
    mn = jnp.min(jnp.where(admissible, sm, BIG), axis=-1, keepdims=True)
    hi = jnp.max(sm, axis=-1, keepdims=True)
    lo = mn - (jnp.abs(mn) + 1.0)

    def step(lo, hi, mid):
        up = _row_count(sm > mid) >= TOP_K
        return jnp.where(up, mid, lo), jnp.where(up, hi, mid)

    def coarse(_, c):
        lo, hi = c
        return step(lo, hi, 0.5 * (lo + hi))

    lo, hi = lax.fori_loop(0, BISECT_ITERS, coarse, (lo, hi))

    def bounds(lo, hi):
        t_lo = jnp.min(jnp.where(sm > lo, sm, BIG), axis=-1, keepdims=True)
        t_hi = jnp.max(jnp.where(sm <= hi, sm, -BIG), axis=-1, keepdims=True)
        return t_lo, t_hi

    def unfinished(c):
        _, _, t_lo, t_hi = c
        return _any((t_lo != t_hi) & search_row)

    def refine(c):
        lo, hi, t_lo, t_hi = c
        mid = 0.5 * (t_lo + t_hi)
        mid = jnp.where(mid >= t_hi, t_lo, mid)
        lo, hi = step(lo, hi, mid)
        return (lo, hi) + bounds(lo, hi)

    _, _, _, thr = lax.while_loop(unfinished, refine, (lo, hi) + bounds(lo, hi))

    above = sm > thr
    tied = sm == thr
    need = TOP_K - _row_count(above)

    def last_tie_column():
        def tie_step(_, c):
            lo_i, hi_i = c
            mid_i = (lo_i + hi_i) >> 1
            ok = _row_count(tied & (col <= mid_i)) >= need
            return jnp.where(ok, lo_i, mid_i), jnp.where(ok, mid_i, hi_i)

        n = sm.shape[1]
        init = (jnp.full(thr.shape, -1, jnp.int32), jnp.full(thr.shape, n - 1, jnp.int32))
        return lax.fori_loop(0, n.bit_length(), tie_step, init)[1]

    overflow = _any((_row_count(tied) > need) & search_row)
    limit = lax.cond(overflow, last_tie_column,
                     lambda: jnp.full(thr.shape, sm.shape[1], jnp.int32))
    return above | (tied & (col <= limit))


def _dsa_block(e, always_search, i, aq_ref, ak_ref, av_ref, iq_ref, ik_ref, iw_ref, o_ref, bias_s):
    tq = bias_s.shape[0]
    low_half = _low_half()
    row = i * tq + lax.broadcasted_iota(jnp.int32, (tq, 1), 0)
    col = lax.broadcasted_iota(jnp.int32, (1, e), 1)
    q_chunk = row // CHUNK
    admissible = (col // CHUNK <= q_chunk) & (col >= PAD)

    iw = iw_ref[0]
    ik = ik_ref[0, 0:e, :]
    score = jnp.zeros((tq, e), F32)
    for h in range(IDX_HEADS):
        pair, half = divmod(h, 2)
        iq = _keep_head(iq_ref[0, :, pair * LANES:(pair + 1) * LANES], half, low_half)
        logits = lax.dot_general(iq, ik, _NT, preferred_element_type=F32)
        score = score + iw[:, h:h + 1] * jnp.maximum(logits, 0.0)
    sm = jnp.where(admissible, score, NEG)

    search_row = N_META + CHUNK * q_chunk >= TOP_K

    def searched():
        keep = admissible & (_topk_select(sm, admissible, search_row, col)
                             | jnp.logical_not(search_row))
        bias_s[:, 0:e] = jnp.where(keep, 0.0, NEG)

    if always_search:
        searched()
    else:
        first_search_block = (TOP_K - N_META + CHUNK - 1) // CHUNK * CHUNK // tq
        pl.when(i >= first_search_block)(searched)

        @pl.when(i < first_search_block)
        def _():
            bias_s[:, 0:e] = jnp.where(admissible, 0.0, NEG)

    bias = bias_s[:, 0:e]
    for pair in range(A_HEADS // 2):
        sl = slice(pair * LANES, (pair + 1) * LANES)
        k = ak_ref[0, 0:e, sl]
        v = av_ref[0, 0:e, sl]
        outs = []
        for half in range(2):
            q = _keep_head(aq_ref[0, :, sl], half, low_half)
            s = lax.dot_general(q, k, _NT, preferred_element_type=F32) + bias
            p = jnp.exp2(s - jnp.max(s, axis=-1, keepdims=True))
            denom = jnp.sum(p, axis=-1, keepdims=True)
            outs.append(jnp.dot(p.astype(BF16), v, preferred_element_type=F32) / denom)
        out = jnp.where(low_half, outs[0], outs[1])
        o_ref[0, :, sl] = jnp.where(row >= PAD, out, 0.0).astype(BF16)


def _dsa_kernel(aq_ref, ak_ref, av_ref, iq_ref, ik_ref, iw_ref, o_ref, bias_s):
    i = pl.program_id(1)
    tq, lp = bias_s.shape
    blocks_per_step = KEY_STEP // tq
    refs = (aq_ref, ak_ref, av_ref, iq_ref, ik_ref, iw_ref, o_ref, bias_s)
    for c in range(-(-lp // KEY_STEP)):
        e = min((c + 1) * KEY_STEP, lp)
        in_class = (i >= c * blocks_per_step) & (i < (c + 1) * blocks_per_step)
        pl.when(in_class)(functools.partial(_dsa_block, e, c > 0, i, *refs))


def _dsa(aq, ak, av, iq, ik, iw):
    b, lp, width = aq.shape
    tq = SEQ_TILE
    q_spec = lambda n: pl.BlockSpec((1, tq, n), lambda bi, i: (bi, i, 0))
    full_spec = lambda n: pl.BlockSpec((1, lp, n), lambda bi, i: (bi, 0, 0))
    return pl.pallas_call(
        _dsa_kernel,
        grid=(b, lp // tq),
        in_specs=[q_spec(width), full_spec(width), full_spec(width), q_spec(width),
                  full_spec(LANES), q_spec(LANES)],
        out_specs=q_spec(width),
        out_shape=jax.ShapeDtypeStruct((b, lp, width), BF16),
        scratch_shapes=[pltpu.VMEM((tq, lp), F32)],
        compiler_params=_params(("parallel", "arbitrary")),
        name="dsa_attention",
    )(aq, ak, av, iq, ik, iw)


def _retention_kernel(q_ref, k_ref, v_ref, g_ref, dmat_ref, qdec_ref, kdec_ref, cdec_ref, o_ref,
                      state):
    i = pl.program_id(1)

    @pl.when(i == 0)
    def _():
        state[...] = jnp.zeros(state.shape, F32)

    low_half = _low_half()
    for pair in range(B_HEADS // 2):
        sl = slice(pair * LANES, (pair + 1) * LANES)
        q_pair = q_ref[0, :, sl]
        k_pair = k_ref[0, :, sl]
        s_old = state[pair]
        s_new = s_old * cdec_ref[pair]
        for half in range(2):
            h = 2 * pair + half
            vsl = slice(h * B_VDIM, (h + 1) * B_VDIM)
            v = v_ref[0, :, vsl]
            q = _keep_head(q_pair, half, low_half)
            inner = lax.dot_general(q, k_pair, _NT, preferred_element_type=F32) * dmat_ref[h]
            y = jnp.dot(inner.astype(BF16), v, preferred_element_type=F32)
            y = y + jnp.dot(q, s_old.astype(BF16), preferred_element_type=F32) * qdec_ref[h]
            kd = _keep_head((k_pair.astype(F32) * kdec_ref[h]).astype(BF16), half, low_half)
            s_new = s_new + lax.dot_general(kd, v, _TN, preferred_element_type=F32)
            o_ref[0, :, vsl] = (_rms(y) * jax.nn.silu(g_ref[0, :, vsl])).astype(BF16)
        state[pair] = s_new


def _retention(q, k, v, gate):
    b, lp, _ = q.shape
    t = RET_TILE
    log_g = jnp.log(1.0 - 2.0 ** (-5.0 - jnp.arange(B_HEADS, dtype=F32)))
    idx = jnp.arange(t, dtype=F32)
    rel = idx[:, None] - idx[None, :]
    dmat = jnp.where(rel >= 0, jnp.exp(jnp.maximum(rel, 0.0)[None] * log_g[:, None, None]), 0.0)
    qdec = jnp.exp((idx + 1.0)[None, :, None] * log_g[:, None, None])
    kdec = jnp.exp((t - 1.0 - idx)[None, :, None] * log_g[:, None, None])
    cdec = jnp.repeat(jnp.exp(t * log_g), HEAD_DIM).reshape(B_HEADS // 2, LANES, 1)

    qk_w, v_w = B_HEADS * HEAD_DIM, B_HEADS * B_VDIM
    seq = lambda n: pl.BlockSpec((1, t, n), lambda bi, i: (bi, i, 0))
    return pl.pallas_call(
        _retention_kernel,
        grid=(b, lp // t),
        in_specs=[seq(qk_w), seq(qk_w), seq(v_w), seq(v_w),
                  _resident((B_HEADS, t, t)), _resident((B_HEADS, t, 1)),
                  _resident((B_HEADS, t, 1)), _resident((B_HEADS // 2, LANES, 1))],
        out_specs=seq(v_w),
        out_shape=jax.ShapeDtypeStruct((b, lp, v_w), BF16),
        scratch_shapes=[pltpu.VMEM((B_HEADS // 2, LANES, B_VDIM), F32)],
        compiler_params=_params(("parallel", "arbitrary")),
        name="retention",
    )(q, k, v, gate, dmat, qdec, kdec, cdec)


def _swa_kernel(q_ref, k_ref, v_ref, sink_ref, o_ref):
    i = pl.program_id(1)
    tq = q_ref.shape[1]
    chunks = tq // CHUNK
    win = (chunks + C_WIN_CHUNKS) * CHUNK
    rep = C_HEADS // C_KV_HEADS
    low_half = _low_half()

    first_chunk = jnp.maximum(i * chunks - C_WIN_CHUNKS, 0)
    start = pl.multiple_of(first_chunk * CHUNK, CHUNK)
    row = i * tq + lax.broadcasted_iota(jnp.int32, (tq, 1), 0)
    q_chunk = jnp.concatenate([row // CHUNK] * rep, axis=0)
    col = lax.broadcasted_iota(jnp.int32, (1, CHUNK + win), 1)
    k_chunk = first_chunk + (col - CHUNK) // CHUNK
    in_window = ((col >= CHUNK) & (k_chunk >= 1) & (k_chunk <= q_chunk)
                 & (k_chunk >= q_chunk - C_WIN_CHUNKS))
    valid = in_window | ((col < CHUNK) & (col >= PAD))

    for g in range(C_KV_HEADS):
        sl = slice(g * LANES, (g + 1) * LANES)
        keys = jnp.concatenate([k_ref[0, 0:CHUNK, sl], k_ref[0, pl.ds(start, win), sl]], axis=0)
        vals = jnp.concatenate([v_ref[0, 0:CHUNK, sl], v_ref[0, pl.ds(start, win), sl]], axis=0)
        heads = range(g * rep, (g + 1) * rep)
        q = jnp.concatenate(
            [_keep_head(q_ref[0, :, (h // 2) * LANES:(h // 2 + 1) * LANES], h % 2, low_half)
             for h in heads], axis=0)
        sink = jnp.concatenate(
            [jnp.broadcast_to(sink_ref[0:1, h:h + 1] * LOG2E, (tq, 1)) for h in heads], axis=0)
        s = jnp.where(valid, lax.dot_general(q, keys, _NT, preferred_element_type=F32), NEG)
        m = jnp.maximum(jnp.max(s, axis=-1, keepdims=True), sink)
        p = jnp.exp2(s - m)
        denom = jnp.sum(p, axis=-1, keepdims=True) + jnp.exp2(sink - m)
        y = jnp.dot(p.astype(BF16), vals, preferred_element_type=F32) / denom
        for j in range(rep // 2):
            out = jnp.where(low_half, y[2 * j * tq:(2 * j + 1) * tq], y[(2 * j + 1) * tq:(2 * j + 2) * tq])
            pair = g * rep // 2 + j
            o_ref[0, :, pair * LANES:(pair + 1) * LANES] = jnp.where(row >= PAD, out, 0.0).astype(BF16)


def _swa(q, k, v, sinks):
    b, lp, width = q.shape
    tq = SEQ_TILE
    kv = pl.BlockSpec((1, lp, k.shape[2]), lambda bi, i: (bi, 0, 0))
    return pl.pallas_call(
        _swa_kernel,
        grid=(b, lp // tq),
        in_specs=[pl.BlockSpec((1, tq, width), lambda bi, i: (bi, i, 0)), kv, kv,
                  _resident((1, C_HEADS))],
        out_specs=pl.BlockSpec((1, tq, width), lambda bi, i: (bi, i, 0)),
        out_shape=jax.ShapeDtypeStruct((b, lp, width), BF16),
        compiler_params=_params(("parallel", "parallel")),
        name="swa_sinks",
    )(q, k, v, sinks.reshape(1, C_HEADS))


def _pool_kernel(x_ref, mix_ref, scale_ref, o_ref):
    lp = x_ref.shape[1]
    row = lax.broadcasted_iota(jnp.int32, (lp, 1), 0)
    t = row - PAD
    outs = []
    for gi, w in enumerate(POOL_WINDOWS):
        x = x_ref[0, :, gi * D_GROUP:(gi + 1) * D_GROUP]
        s, shift = x, 1
        while shift < w:
            s = s + jnp.where(row >= shift, pltpu.roll(s, shift, 0), 0.0)
            shift *= 2
        count = jnp.maximum(jnp.minimum(t + 1, w), 1).astype(F32)
        y = (s / count - x).astype(BF16)
        outs.append(jnp.dot(y, mix_ref[gi].astype(BF16), preferred_element_type=F32))
    out = jnp.concatenate(outs, axis=-1) * scale_ref[...]
    o_ref[0] = jnp.where(row >= PAD, out, 0.0).astype(BF16)


def _pool(x, mix, scale):
    b, lp, width = x.shape
    n_groups = len(POOL_WINDOWS)
    return pl.pallas_call(
        _pool_kernel,
        grid=(b,),
        in_specs=[pl.BlockSpec((1, lp, width), lambda bi: (bi, 0, 0)),
                  _resident((n_groups, D_GROUP, D_GROUP)), _resident((1, width))],
        out_specs=pl.BlockSpec((1, lp, width), lambda bi: (bi, 0, 0)),
        out_shape=jax.ShapeDtypeStruct((b, lp, width), BF16),
        compiler_params=_params(("parallel",)),
        name="pool_mixer",
    )(x, mix, scale.reshape(1, width))


def kernel(x, meta_tokens, ffn1_norm, ffn1_w_in, ffn1_w_out, mix_norm, ffn2_norm, ffn2_w_in,
           ffn2_w_out, ev_w_in, ev_a_q_norm, ev_a_k_norm, ev_w_out, od_w_in, od_c_q_norm,
           od_c_k_norm, od_c_sinks, od_d_mix, od_d_scale, od_w_out):
    b, s, d = x.shape
    lp = PAD + N_META + s
    assert d == D_MODEL and (b * lp) % ROW_TILE == 0
    assert lp % SEQ_TILE == 0 and lp % PROJ_TILE == 0 and lp % RET_TILE == 0
    assert KEY_STEP % SEQ_TILE == 0 and min(TOP_K, s // 4) == TOP_K
    depth = ffn1_norm.shape[0]

    h = jnp.concatenate([jnp.zeros((b, PAD, d), x.dtype),
                         jnp.broadcast_to(meta_tokens.astype(x.dtype)[None], (b, N_META, d)), x],
                        axis=1).reshape(b * lp, d)
    seq = lambda a: a.reshape(b, lp, a.shape[-1])
    flat = lambda a: a.reshape(b * lp, a.shape[-1])

    for layer in range(depth):
        h = _ffn(h, ffn1_norm[layer], ffn1_w_in[layer].astype(BF16), ffn1_w_out[layer].astype(BF16))
        if layer % 2 == 0:
            e = layer // 2
            aq, ak, av, iq, ik, iw, bq, bk, bv, bg = _even_proj(
                seq(h), mix_norm[layer], ev_w_in[e], ev_a_q_norm[e], ev_a_k_norm[e])
            ya = _dsa(aq, ak, av, iq, ik, iw)
            yb = _retention(bq, bk, bv, bg)
            h = _out_proj(h, flat(ya), flat(yb), ev_w_out[e].astype(BF16))
        else:
            o = layer // 2
            cq, ck, cv, dx = _odd_proj(seq(h), mix_norm[layer], od_w_in[o], od_c_q_norm[o],
                                       od_c_k_norm[o])
            yc = _swa(cq, ck, cv, od_c_sinks[o])
            yd = _pool(dx, od_d_mix[o], od_d_scale[o])
            h = _out_proj(h, flat(yc), flat(yd), od_w_out[o].astype(BF16))
        h = _ffn(h, ffn2_norm[layer], ffn2_w_in[layer].astype(BF16), ffn2_w_out[layer].astype(BF16))
    return h.reshape(b, lp, d)[:, PAD + N_META:]
```

```python
import functools

import jax
import jax.numpy as jnp
from jax import lax
from jax.experimental import pallas as pl
from jax.experimental.pallas import tpu as pltpu

F32 = jnp.float32
BF16 = jnp.bfloat16

D_MODEL = 1024
D_FF = 2816
CHUNK = 64
N_META = 16
PAD = CHUNK - N_META
HEAD_DIM = 64
LANES = 128
A_HEADS = 8
IDX_HEADS = 8
IDX_DIM = 64
TOP_K = 256
B_HEADS = 8
B_VDIM = 128
C_HEADS = 8
C_KV_HEADS = 2
C_WIN_CHUNKS = 2
D_WIDTH = 512
POOL_WINDOWS = (2, 4, 8, 16)
D_GROUP = 128
EPS = 1e-6
NEG = -1e30
BIG = 3e38
LOG2E = 1.4426950408889634

VMEM_LIMIT = 56 * 1024 * 1024
ROW_TILE = 512
FF_CHUNK = 256
PROJ_TILE = 704
KEY_TILE = 128
KEY_UNROLL = 2
RET_TILE = 352
BISECT_ITERS = 20

_NT = (((1,), (1,)), ((), ()))
_TN = (((0,), (0,)), ((), ()))


def _params(sem):
    return pltpu.CompilerParams(dimension_semantics=sem, vmem_limit_bytes=VMEM_LIMIT)


def _rms(x, gain=None):
    y = x * lax.rsqrt(jnp.mean(x * x, axis=-1, keepdims=True) + EPS)
    return y if gain is None else y * gain


def _resident(shape):
    return pl.BlockSpec(shape, lambda *_: (0,) * len(shape), pipeline_mode=pl.Buffered(1))


def _low_half():
    return lax.broadcasted_iota(jnp.int32, (1, LANES), 1) < HEAD_DIM


def _keep_head(x, half, low_half):
    return jnp.where(low_half if half == 0 else jnp.logical_not(low_half), x, jnp.zeros_like(x))


def _ffn_kernel(h_ref, g_ref, win_ref, wout_ref, o_ref):
    x = h_ref[...]
    xn = _rms(x, g_ref[...]).astype(BF16)
    acc = jnp.zeros(x.shape, F32)
    for c in range(D_FF // FF_CHUNK):
        lo = c * FF_CHUNK
        g = jnp.dot(xn, win_ref[:, lo:lo + FF_CHUNK], preferred_element_type=F32)
        u = jnp.dot(xn, win_ref[:, D_FF + lo:D_FF + lo + FF_CHUNK], preferred_element_type=F32)
        a = (jax.nn.silu(g) * u).astype(BF16)
        acc = acc + jnp.dot(a, wout_ref[lo:lo + FF_CHUNK, :], preferred_element_type=F32)
    o_ref[...] = x + 0.5 * acc


def _ffn(h, gain, w_in, w_out):
    rows = h.shape[0]
    return pl.pallas_call(
        _ffn_kernel,
        grid=(rows // ROW_TILE,),
        in_specs=[
            pl.BlockSpec((ROW_TILE, D_MODEL), lambda i: (i, 0)),
            _resident((1, D_MODEL)),
            _resident((D_MODEL, 2 * D_FF)),
            _resident((D_FF, D_MODEL)),
        ],
        out_specs=pl.BlockSpec((ROW_TILE, D_MODEL), lambda i: (i, 0)),
        out_shape=jax.ShapeDtypeStruct((rows, D_MODEL), F32),
        compiler_params=_params(("parallel",)),
        name="ffn",
    )(h, gain.reshape(1, D_MODEL), w_in, w_out)


def _head_norm(y, seg_ref, gain):
    sq = (y * y).astype(BF16)
    ms = jnp.concatenate(
        [jnp.dot(sq[:, p * LANES:(p + 1) * LANES], seg_ref[...], preferred_element_type=F32)
         for p in range(y.shape[1] // LANES)], axis=-1)
    return y * lax.rsqrt(ms + EPS) * gain


def _rotate_half(x, first_half):
    n = x.shape[-1]
    half = HEAD_DIM // 2
    return jnp.where(first_half, -pltpu.roll(x, n - half, 1), pltpu.roll(x, half, 1))


def _even_proj_kernel(h_ref, g_ref, w_ref, seg_ref, qg_ref, kg_ref, cos_ref, sin_ref,
                      aq_ref, ak_ref, av_ref, iq_ref, ik_ref, iw_ref, bq_ref, bk_ref, bv_ref,
                      bg_ref):
    xn = _rms(h_ref[0], g_ref[...]).astype(BF16)
    off = [0]

    def cols(n):
        y = jnp.dot(xn, w_ref[:, off[0]:off[0] + n], preferred_element_type=F32)
        off[0] += n
        return y

    w = A_HEADS * HEAD_DIM
    aq_ref[0] = _head_norm(cols(w), seg_ref, qg_ref[...]).astype(BF16)
    ak_ref[0] = _head_norm(cols(w), seg_ref, kg_ref[...]).astype(BF16)
    av_ref[0] = cols(w).astype(BF16)
    iq_ref[0] = cols(IDX_HEADS * IDX_DIM).astype(BF16)
    ik_ref[0] = cols(LANES).astype(BF16)
    iw_ref[0] = cols(LANES) * (IDX_HEADS ** -0.5 * IDX_DIM ** -0.5)
    cos, sin = cos_ref[...], sin_ref[...]
    first_half = lax.broadcasted_iota(jnp.int32, (1, w), 1) % HEAD_DIM < HEAD_DIM // 2
    q = cols(w)
    bq_ref[0] = (q * cos + _rotate_half(q, first_half) * sin).astype(BF16)
    k = cols(w)
    bk_ref[0] = ((k * cos + _rotate_half(k, first_half) * sin) * HEAD_DIM ** -0.5).astype(BF16)
    for c in range(2):
        sl = slice(c * w, (c + 1) * w)
        bv_ref[0, :, sl] = cols(w).astype(BF16)
    for c in range(2):
        sl = slice(c * w, (c + 1) * w)
        bg_ref[0, :, sl] = cols(w)


def _odd_proj_kernel(h_ref, g_ref, w_ref, seg_ref, qg_ref, kg_ref, cq_ref, ck_ref, cv_ref, dx_ref):
    xn = _rms(h_ref[0], g_ref[...]).astype(BF16)
    off = [0]

    def cols(n):
        y = jnp.dot(xn, w_ref[:, off[0]:off[0] + n], preferred_element_type=F32)
        off[0] += n
        return y

    kv_w = 2 * C_KV_HEADS * HEAD_DIM
    cq_ref[0] = _head_norm(cols(C_HEADS * HEAD_DIM), seg_ref, qg_ref[...]).astype(BF16)
    ck_ref[0] = _head_norm(cols(kv_w), seg_ref, kg_ref[...]).astype(BF16)
    cv_ref[0] = cols(kv_w).astype(BF16)
    dx_ref[0] = cols(D_WIDTH)


def _seg_matrix():
    g = jnp.arange(LANES) // HEAD_DIM
    return jnp.where(g[:, None] == g[None, :], 1.0 / HEAD_DIM, 0.0).astype(BF16)


def _proj_call(body, name, h, gain, w, extra_in, extra_specs, outs):
    b, lp, _ = h.shape
    t = PROJ_TILE
    seq = lambda n: pl.BlockSpec((1, t, n), lambda bi, i: (bi, i, 0))
    return pl.pallas_call(
        body,
        grid=(b, lp // t),
        in_specs=[seq(D_MODEL), _resident((1, D_MODEL)), _resident(w.shape),
                  _resident((LANES, LANES))] + extra_specs,
        out_specs=[seq(n) for n, _ in outs],
        out_shape=[jax.ShapeDtypeStruct((b, lp, n), dt) for n, dt in outs],
        compiler_params=_params(("parallel", "parallel")),
        name=name,
    )(h, gain.reshape(1, D_MODEL), w, _seg_matrix(), *extra_in)


def _rotary_tables(lp):
    pos = (jnp.arange(lp) - PAD).astype(F32)
    inv = 1.0 / (10000.0 ** (jnp.arange(0, HEAD_DIM, 2, dtype=F32) / HEAD_DIM))
    ang = pos[:, None] * inv[None]
    return (jnp.tile(jnp.cos(ang), (1, 2 * B_HEADS)), jnp.tile(jnp.sin(ang), (1, 2 * B_HEADS)))


def _even_proj(h, gain, w, q_gain, k_gain):
    lp = h.shape[1]
    a_w = 3 * A_HEADS * HEAD_DIM + IDX_HEADS * IDX_DIM
    ik = w[:, a_w:a_w + IDX_DIM]
    iw = jnp.pad(w[:, a_w + IDX_DIM:a_w + IDX_DIM + IDX_HEADS], ((0, 0), (0, LANES - IDX_HEADS)))
    w = jnp.concatenate([w[:, :a_w], ik, ik, iw, w[:, a_w + IDX_DIM + IDX_HEADS:]], axis=1)
    cos, sin = _rotary_tables(lp)
    w_heads = A_HEADS * HEAD_DIM
    qg = jnp.tile(q_gain, A_HEADS).reshape(1, w_heads) * (HEAD_DIM ** -0.5 * LOG2E)
    kg = jnp.tile(k_gain, A_HEADS).reshape(1, w_heads)
    table = pl.BlockSpec((PROJ_TILE, w_heads), lambda bi, i: (i, 0))
    outs = [(w_heads, BF16)] * 4 + [(LANES, BF16), (LANES, F32), (w_heads, BF16), (w_heads, BF16),
                                    (B_HEADS * B_VDIM, BF16), (B_HEADS * B_VDIM, F32)]
    return _proj_call(_even_proj_kernel, "even_in_proj", h, gain, w.astype(BF16),
                      [qg, kg, cos, sin],
                      [_resident((1, w_heads)), _resident((1, w_heads)), table, table], outs)


def _odd_proj(h, gain, w, q_gain, k_gain):
    q_w = C_HEADS * HEAD_DIM
    kv_w = C_KV_HEADS * HEAD_DIM
    dup = lambda m: jnp.concatenate(
        [m[:, g * HEAD_DIM:(g + 1) * HEAD_DIM] for g in range(C_KV_HEADS) for _ in range(2)], axis=1)
    w = jnp.concatenate([w[:, :q_w], dup(w[:, q_w:q_w + kv_w]), dup(w[:, q_w + kv_w:q_w + 2 * kv_w]),
                         w[:, q_w + 2 * kv_w:]], axis=1)
    qg = jnp.tile(q_gain, C_HEADS).reshape(1, q_w) * (HEAD_DIM ** -0.5 * LOG2E)
    kg = jnp.tile(k_gain, 2 * C_KV_HEADS).reshape(1, 2 * kv_w)
    outs = [(q_w, BF16), (2 * kv_w, BF16), (2 * kv_w, BF16), (D_WIDTH, F32)]
    return _proj_call(_odd_proj_kernel, "odd_in_proj", h, gain, w.astype(BF16), [qg, kg],
                      [_resident((1, q_w)), _resident((1, 2 * kv_w))], outs)


def _out_proj_kernel(h_ref, ya_ref, yb_ref, w_ref, o_ref):
    na = ya_ref.shape[1]
    y = jnp.dot(ya_ref[...], w_ref[0:na, :], preferred_element_type=F32)
    y = y + jnp.dot(yb_ref[...], w_ref[na:, :], preferred_element_type=F32)
    o_ref[...] = h_ref[...] + y


def _out_proj(h, ya, yb, w):
    rows = h.shape[0]
    na, nb = ya.shape[1], yb.shape[1]
    return pl.pallas_call(
        _out_proj_kernel,
        grid=(rows // ROW_TILE,),
        in_specs=[
            pl.BlockSpec((ROW_TILE, D_MODEL), lambda i: (i, 0)),
            pl.BlockSpec((ROW_TILE, na), lambda i: (i, 0)),
            pl.BlockSpec((ROW_TILE, nb), lambda i: (i, 0)),
            _resident((na + nb, D_MODEL)),
        ],
        out_specs=pl.BlockSpec((ROW_TILE, D_MODEL), lambda i: (i, 0)),
        out_shape=jax.ShapeDtypeStruct((rows, D_MODEL), F32),
        compiler_params=_params(("parallel",)),
        name="mixer_out_proj",
    )(h, ya, yb, w)


def _fold(x):
    return x.reshape(x.shape[0] // 8, 8, x.shape[1]).sum(axis=0)


def _any(mask):
    return jnp.max(jnp.where(mask, 1.0, 0.0)) > 0.0


def _topk_threshold(sc_s, n_groups, lo, hi, search):
    kt = sc_s.shape[1]
    nq = sc_s.shape[2]
    key_in_tile = lax.broadcasted_iota(jnp.int32, (kt, 1), 0)

    def scan(fn, init):
        def body(g, acc):
            for u in range(KEY_UNROLL):
                c = g * KEY_UNROLL + u
                acc = fn(acc, sc_s[c], c)
            return acc
        return lax.fori_loop(0, n_groups, body, init)

    def count(pred):
        acc = scan(lambda a, s, c: a + _fold(jnp.where(pred(s, c), 1.0, 0.0)), jnp.zeros((8, nq), F32))
        return jnp.sum(acc, axis=0, keepdims=True)

    def step(lo, hi, mid):
        up = count(lambda s, c: s > mid) >= TOP_K
        return jnp.where(up, mid, lo), jnp.where(up, hi, mid)

    def coarse(_, c):
        lo, hi = c
        return step(lo, hi, 0.5 * (lo + hi))

    lo, hi = lax.fori_loop(0, BISECT_ITERS, coarse, (lo, hi))

    def bounds(lo, hi):
        def fn(acc, s, c):
            t_lo, t_hi = acc
            t_lo = jnp.minimum(t_lo, jnp.min(jnp.where(s > lo, s, BIG), axis=0, keepdims=True))
            t_hi = jnp.maximum(t_hi, jnp.max(jnp.where(s <= hi, s, -BIG), axis=0, keepdims=True))
            return t_lo, t_hi
        return scan(fn, (jnp.full((1, nq), BIG, F32), jnp.full((1, nq), -BIG, F32)))

    def unfinished(c):
        _, _, t_lo, t_hi = c
        return _any((t_lo != t_hi) & search)

    def refine(c):
        lo, hi, t_lo, t_hi = c
        mid = 0.5 * (t_lo + t_hi)
        mid = jnp.where(mid >= t_hi, t_lo, mid)
        lo, hi = step(lo, hi, mid)
        return (lo, hi) + bounds(lo, hi)

    _, _, _, thr = lax.while_loop(unfinished, refine, (lo, hi) + bounds(lo, hi))

    need = TOP_K - count(lambda s, c: s > thr)
    n_keys = kt * sc_s.shape[0]

    def last_tie_key():
        def tie_step(_, c):
            lo_i, hi_i = c
            mid_i = (lo_i + hi_i) >> 1
            ok = count(lambda s, c: (s == thr) & (c * kt + key_in_tile <= mid_i)) >= need
            return jnp.where(ok, lo_i, mid_i), jnp.where(ok, mid_i, hi_i)

        init = (jnp.full((1, nq), -1, jnp.int32), jnp.full((1, nq), n_keys - 1, jnp.int32))
        return lax.fori_loop(0, n_keys.bit_length(), tie_step, init)[1]

    overflow = _any((count(lambda s, c: s == thr) > need) & search)
    limit = lax.cond(overflow, last_tie_key, lambda: jnp.full((1, nq), n_keys, jnp.int32))
    return thr, limit


def _dsa_kernel(aq_ref, iq_ref, iw_ref, ak_ref, av_ref, ik_ref, o_ref, k_s, vt_s, ik_s, sc_s, acc_s):
    j = pl.program_id(1)
    lp = ak_ref.shape[1]
    kt = KEY_TILE
    n_full, tail = divmod(lp, kt)
    pairs = A_HEADS // 2

    @pl.when(j == 0)
    def _():
        def put(c, k, ik, v):
            k_s[c] = k
            ik_s[c] = ik
            v = v.astype(F32)
            for p in range(pairs):
                vt_s[c, p] = v[:, p * LANES:(p + 1) * LANES].T.astype(BF16)

        def copy_tile(c, carry):
            rows = pl.ds(pl.multiple_of(c * kt, kt), kt)
            put(c, ak_ref[0, rows, :], ik_ref[0, rows, :], av_ref[0, rows, :])
            return carry

        lax.fori_loop(0, n_full, copy_tile, 0)
        if tail:
            padded = lambda a: jnp.concatenate(
                [a, jnp.zeros((kt - tail, a.shape[1]), a.dtype)], axis=0)
            rows = slice(n_full * kt, lp)
            put(n_full, padded(ak_ref[0, rows, :]), padded(ik_ref[0, rows, :]),
                padded(av_ref[0, rows, :]))
        for c in range(n_full + (tail > 0), k_s.shape[0]):
            put(c, jnp.zeros(k_s.shape[1:], BF16), jnp.zeros(ik_s.shape[1:], BF16),
                jnp.zeros(k_s.shape[1:], BF16))
        sc_s[...] = jnp.full(sc_s.shape, NEG, F32)

    low_half = _low_half()
    q_row = j * kt + lax.broadcasted_iota(jnp.int32, (kt, 1), 0)
    q_col = j * kt + lax.broadcasted_iota(jnp.int32, (1, kt), 1)
    in_range = q_row < lp
    aq = jnp.where(in_range, aq_ref[0], jnp.zeros_like(aq_ref[0]))
    iq = jnp.where(in_range, iq_ref[0], jnp.zeros_like(iq_ref[0]))
    iw_t = jnp.where(in_range, iw_ref[0], 0.0).T

    def both_heads(x, p):
        x = x[:, p * LANES:(p + 1) * LANES]
        return jnp.concatenate([_keep_head(x, 0, low_half), _keep_head(x, 1, low_half)], axis=0)

    iq_p = [both_heads(iq, p) for p in range(pairs)]
    aq_p = [both_heads(aq, p) for p in range(pairs)]
    key_in_tile = lax.broadcasted_iota(jnp.int32, (kt, 1), 0)
    key_grid = lax.broadcasted_iota(jnp.int32, (kt, kt), 0)
    key_limit = jnp.minimum((q_col // CHUNK + 1) * CHUNK, lp)
    n_groups = (j + KEY_UNROLL) // KEY_UNROLL

    def score_tiles(g, carry):
        mn, mx = carry
        tiles = [g * KEY_UNROLL + u for u in range(KEY_UNROLL)]
        logits = [[lax.dot_general(ik_s[c], iq_p[p], _NT, preferred_element_type=F32)
                   for p in range(pairs)] for c in tiles]
        for c, lg in zip(tiles, logits):
            sc = jnp.zeros((kt, kt), F32)
            for p in range(pairs):
                for half in range(2):
                    h = 2 * p + half
                    sc = sc + iw_t[h:h + 1, :] * jnp.maximum(lg[p][:, half * kt:(half + 1) * kt], 0.0)
            key = key_grid + c * kt
            admissible = (key >= jnp.where(c == 0, PAD, 0)) & (key < key_limit)
            sc_s[c] = jnp.where(admissible, sc, NEG)
            mn = jnp.minimum(mn, jnp.min(jnp.where(admissible, sc, BIG), axis=0, keepdims=True))
            mx = jnp.maximum(mx, jnp.max(jnp.where(admissible, sc, NEG), axis=0, keepdims=True))
        return mn, mx

    mn, mx = lax.fori_loop(0, n_groups, score_tiles,
                           (jnp.full((1, kt), BIG, F32), jnp.full((1, kt), NEG, F32)))

    first_search_block = (TOP_K - N_META + CHUNK - 1) // CHUNK * CHUNK // kt

    def to_bias(keep):
        def body(g, carry):
            for u in range(KEY_UNROLL):
                c = g * KEY_UNROLL + u
                sc_s[c] = jnp.where(keep(sc_s[c], c), 0.0, NEG)
            return carry
        lax.fori_loop(0, n_groups, body, 0)

    @pl.when(j < first_search_block)
    def _():
        to_bias(lambda s, c: s > 0.5 * NEG)

    @pl.when(j >= first_search_block)
    def _():
        search = q_col < lp
        thr, limit = _topk_threshold(sc_s, n_groups, mn - (jnp.abs(mn) + 1.0), mx, search)
        to_bias(lambda s, c: (s > thr) | ((s == thr) & (c * kt + key_in_tile <= limit)))

    acc_s[...] = jnp.zeros(acc_s.shape, F32)

    def attend(g, carry):
        m, l = carry
        c0 = g * KEY_UNROLL
        bias = jnp.concatenate([sc_s[c0 + u] for u in range(KEY_UNROLL)], axis=0)
        bias = jnp.concatenate([bias, bias], axis=1)
        s = []
        for p in range(pairs):
            k = jnp.concatenate([k_s[c0 + u, :, p * LANES:(p + 1) * LANES] for u in range(KEY_UNROLL)],
                                axis=0)
            s.append(lax.dot_general(k, aq_p[p], _NT, preferred_element_type=F32) + bias)
        new_m, new_l = [], []
        for p in range(pairs):
            m_p = jnp.maximum(m[p], jnp.max(s[p], axis=0, keepdims=True))
            alpha = jnp.exp2(m[p] - m_p)
            pr = jnp.exp2(s[p] - m_p)
            new_l.append(alpha * l[p] + jnp.sum(pr, axis=0, keepdims=True))
            vt = jnp.concatenate([vt_s[c0 + u, p] for u in range(KEY_UNROLL)], axis=1)
            acc_s[p] = alpha * acc_s[p] + jnp.dot(vt, pr.astype(BF16), preferred_element_type=F32)
            new_m.append(m_p)
        return tuple(new_m), tuple(new_l)

    init = (tuple(jnp.full((1, 2 * kt), -BIG, F32) for _ in range(pairs)),
            tuple(jnp.zeros((1, 2 * kt), F32) for _ in range(pairs)))
    _, l = lax.fori_loop(0, n_groups, attend, init)

    v_row_low = lax.broadcasted_iota(jnp.int32, (LANES, 1), 0) < HEAD_DIM
    for p in range(pairs):
        out_t = acc_s[p] / l[p]
        out_t = jnp.where(v_row_low, out_t[:, 0:kt], out_t[:, kt:2 * kt])
        o_ref[0, :, p * LANES:(p + 1) * LANES] = jnp.where(q_row >= PAD, out_t.T, 0.0).astype(BF16)


def _dsa(aq, ak, av, iq, ik, iw):
    b, lp, width = aq.shape
    kt = KEY_TILE
    n_tiles = -(-lp // kt)
    n_scan = -(-n_tiles // KEY_UNROLL) * KEY_UNROLL
    q_spec = lambda n: pl.BlockSpec((1, kt, n), lambda bi, j: (bi, j, 0))
    full_spec = lambda n: pl.BlockSpec((1, lp, n), lambda bi, j: (bi, 0, 0))
    return pl.pallas_call(
        _dsa_kernel,
        grid=(b, n_tiles),
        in_specs=[q_spec(width), q_spec(width), q_spec(LANES), full_spec(width), full_spec(width),
                  full_spec(LANES)],
        out_specs=q_spec(width),
        out_shape=jax.ShapeDtypeStruct((b, lp, width), BF16),
        scratch_shapes=[
            pltpu.VMEM((n_scan, kt, width), BF16),
            pltpu.VMEM((n_scan, A_HEADS // 2, LANES, kt), BF16),
            pltpu.VMEM((n_scan, kt, LANES), BF16),
            pltpu.VMEM((n_scan, kt, kt), F32),
            pltpu.VMEM((A_HEADS // 2, LANES, 2 * kt), F32),
        ],
        compiler_params=_params(("parallel", "arbitrary")),
        name="dsa_attention",
    )(aq, iq, iw, ak, av, ik)


def _retention_kernel(q_ref, k_ref, v_ref, g_ref, dmat_ref, qdec_ref, kdec_ref, cdec_ref, o_ref,
                      state):
    i = pl.program_id(1)

    @pl.when(i == 0)
    def _():
        state[...] = jnp.zeros(state.shape, F32)

    low_half = _low_half()
    for pair in range(B_HEADS // 2):
        sl = slice(pair * LANES, (pair + 1) * LANES)
        q_pair = q_ref[0, :, sl]
        k_pair = k_ref[0, :, sl]
        s_old = state[pair]
        s_new = s_old * cdec_ref[pair]
        for half in range(2):
            h = 2 * pair + half
            vsl = slice(h * B_VDIM, (h + 1) * B_VDIM)
            v = v_ref[0, :, vsl]
            q = _keep_head(q_pair, half, low_half)
            inner = lax.dot_general(q, k_pair, _NT, preferred_element_type=F32) * dmat_ref[h]
            y = jnp.dot(inner.astype(BF16), v, preferred_element_type=F32)
            y = y + jnp.dot(q, s_old.astype(BF16), preferred_element_type=F32) * qdec_ref[h]
            kd = _keep_head((k_pair.astype(F32) * kdec_ref[h]).astype(BF16), half, low_half)
            s_new = s_new + lax.dot_general(kd, v, _TN, preferred_element_type=F32)
            o_ref[0, :, vsl] = (_rms(y) * jax.nn.silu(g_ref[0, :, vsl])).astype(BF16)
        state[pair] = s_new


def _retention(q, k, v, gate):
    b, lp, _ = q.shape
    t = RET_TILE
    log_g = jnp.log(1.0 - 2.0 ** (-5.0 - jnp.arange(B_HEADS, dtype=F32)))
    idx = jnp.arange(t, dtype=F32)
    rel = idx[:, None] - idx[None, :]
    dmat = jnp.where(rel >= 0, jnp.exp(jnp.maximum(rel, 0.0)[None] * log_g[:, None, None]), 0.0)
    qdec = jnp.exp((idx + 1.0)[None, :, None] * log_g[:, None, None])
    kdec = jnp.exp((t - 1.0 - idx)[None, :, None] * log_g[:, None, None])
    cdec = jnp.repeat(jnp.exp(t * log_g), HEAD_DIM).reshape(B_HEADS // 2, LANES, 1)

    qk_w, v_w = B_HEADS * HEAD_DIM, B_HEADS * B_VDIM
    seq = lambda n: pl.BlockSpec((1, t, n), lambda bi, i: (bi, i, 0))
    return pl.pallas_call(
        _retention_kernel,
        grid=(b, lp // t),
        in_specs=[seq(qk_w), seq(qk_w), seq(v_w), seq(v_w),
                  _resident((B_HEADS, t, t)), _resident((B_HEADS, t, 1)),
                  _resident((B_HEADS, t, 1)), _resident((B_HEADS // 2, LANES, 1))],
        out_specs=seq(v_w),
        out_shape=jax.ShapeDtypeStruct((b, lp, v_w), BF16),
        scratch_shapes=[pltpu.VMEM((B_HEADS // 2, LANES, B_VDIM), F32)],
        compiler_params=_params(("parallel", "arbitrary")),
        name="retention",
    )(q, k, v, gate, dmat, qdec, kdec, cdec)


def _swa_kernel(q_ref, k_ref, v_ref, sink_ref, o_ref):
    j = pl.program_id(1)
    tq = q_ref.shape[1]
    lp = k_ref.shape[1]
    win = tq + C_WIN_CHUNKS * CHUNK
    rep = C_HEADS // C_KV_HEADS
    low_half = _low_half()

    start = pl.multiple_of(jnp.clip(j * tq - C_WIN_CHUNKS * CHUNK, 0, lp - win), CHUNK)
    q_row = j * tq + lax.broadcasted_iota(jnp.int32, (tq, 1), 0)
    q_col = j * tq + lax.broadcasted_iota(jnp.int32, (1, tq), 1)
    q = jnp.where(q_row < lp, q_ref[0], jnp.zeros_like(q_ref[0]))
    q_chunk = jnp.concatenate([q_col // CHUNK] * rep, axis=1)
    key = lax.broadcasted_iota(jnp.int32, (CHUNK + win, 1), 0)
    k_chunk = start // CHUNK + (key - CHUNK) // CHUNK
    in_window = ((key >= CHUNK) & (k_chunk >= 1) & (k_chunk <= q_chunk)
                 & (k_chunk >= q_chunk - C_WIN_CHUNKS))
    valid = in_window | ((key < CHUNK) & (key >= PAD))
    v_row_low = lax.broadcasted_iota(jnp.int32, (LANES, 1), 0) < HEAD_DIM

    for g in range(C_KV_HEADS):
        sl = slice(g * LANES, (g + 1) * LANES)
        keys = jnp.concatenate([k_ref[0, 0:CHUNK, sl], k_ref[0, pl.ds(start, win), sl]], axis=0)
        vals = jnp.concatenate([v_ref[0, 0:CHUNK, sl], v_ref[0, pl.ds(start, win), sl]], axis=0)
        heads = range(g * rep, (g + 1) * rep)
        qs = jnp.concatenate(
            [_keep_head(q[:, (h // 2) * LANES:(h // 2 + 1) * LANES], h % 2, low_half)
             for h in heads], axis=0)
        sink = jnp.concatenate(
            [jnp.broadcast_to(sink_ref[0:1, h:h + 1] * LOG2E, (1, tq)) for h in heads], axis=1)
        s = jnp.where(valid, lax.dot_general(keys, qs, _NT, preferred_element_type=F32), NEG)
        m = jnp.maximum(jnp.max(s, axis=0, keepdims=True), sink)
        p = jnp.exp2(s - m)
        denom = jnp.sum(p, axis=0, keepdims=True) + jnp.exp2(sink - m)
        y_t = lax.dot_general(vals, p.astype(BF16), _TN, preferred_element_type=F32) / denom
        for u in range(rep // 2):
            out_t = jnp.where(v_row_low, y_t[:, 2 * u * tq:(2 * u + 1) * tq],
                              y_t[:, (2 * u + 1) * tq:(2 * u + 2) * tq])
            pair = g * rep // 2 + u
            o_ref[0, :, pair * LANES:(pair + 1) * LANES] = jnp.where(
                q_row >= PAD, out_t.T, 0.0).astype(BF16)


def _swa(q, k, v, sinks):
    b, lp, width = q.shape
    tq = KEY_TILE
    kv = pl.BlockSpec((1, lp, k.shape[2]), lambda bi, j: (bi, 0, 0))
    return pl.pallas_call(
        _swa_kernel,
        grid=(b, -(-lp // tq)),
        in_specs=[pl.BlockSpec((1, tq, width), lambda bi, j: (bi, j, 0)), kv, kv,
                  _resident((1, C_HEADS))],
        out_specs=pl.BlockSpec((1, tq, width), lambda bi, j: (bi, j, 0)),
        out_shape=jax.ShapeDtypeStruct((b, lp, width), BF16),
        compiler_params=_params(("parallel", "parallel")),
        name="swa_sinks",
    )(q, k, v, sinks.reshape(1, C_HEADS))


def _pool_kernel(x_ref, mix_ref, scale_ref, o_ref):
    lp = x_ref.shape[1]
    row = lax.broadcasted_iota(jnp.int32, (lp, 1), 0)
    t = row - PAD
    outs = []
    for gi, w in enumerate(POOL_WINDOWS):
        x = x_ref[0, :, gi * D_GROUP:(gi + 1) * D_GROUP]
        s, shift = x, 1
        while shift < w:
            s = s + jnp.where(row >= shift, pltpu.roll(s, shift, 0), 0.0)
            shift *= 2
        count = jnp.maximum(jnp.minimum(t + 1, w), 1).astype(F32)
        y = (s / count - x).astype(BF16)
        outs.append(jnp.dot(y, mix_ref[gi].astype(BF16), preferred_element_type=F32))
    out = jnp.concatenate(outs, axis=-1) * scale_ref[...]
    o_ref[0] = jnp.where(row >= PAD, out, 0.0).astype(BF16)


def _pool(x, mix, scale):
    b, lp, width = x.shape
    n_groups = len(POOL_WINDOWS)
    return pl.pallas_call(
        _pool_kernel,
        grid=(b,),
        in_specs=[pl.BlockSpec((1, lp, width), lambda bi: (bi, 0, 0)),
                  _resident((n_groups, D_GROUP, D_GROUP)), _resident((1, width))],
        out_specs=pl.BlockSpec((1, lp, width), lambda bi: (bi, 0, 0)),
        out_shape=jax.ShapeDtypeStruct((b, lp, width), BF16),
        compiler_params=_params(("parallel",)),
        name="pool_mixer",
    )(x, mix, scale.reshape(1, width))


def kernel(x, meta_tokens, ffn1_norm, ffn1_w_in, ffn1_w_out, mix_norm, ffn2_norm, ffn2_w_in,
           ffn2_w_out, ev_w_in, ev_a_q_norm, ev_a_k_norm, ev_w_out, od_w_in, od_c_q_norm,
           od_c_k_norm, od_c_sinks, od_d_mix, od_d_scale, od_w_out):
    b, s, d = x.shape
    lp = PAD + N_META + s
    assert d == D_MODEL and (b * lp) % ROW_TILE == 0
    assert lp % PROJ_TILE == 0 and lp % RET_TILE == 0
    assert min(TOP_K, s // 4) == TOP_K
    depth = ffn1_norm.shape[0]

    h = jnp.concatenate([jnp.zeros((b, PAD, d), x.dtype),
                         jnp.broadcast_to(meta_tokens.astype(x.dtype)[None], (b, N_META, d)), x],
                        axis=1).reshape(b * lp, d)
    seq = lambda a: a.reshape(b, lp, a.shape[-1])
    flat = lambda a: a.reshape(b * lp, a.shape[-1])

    for layer in range(depth):
        h = _ffn(h, ffn1_norm[layer], ffn1_w_in[layer].astype(BF16), ffn1_w_out[layer].astype(BF16))
        if layer % 2 == 0:
            e = layer // 2
            aq, ak, av, iq, ik, iw, bq, bk, bv, bg = _even_proj(
                seq(h), mix_norm[layer], ev_w_in[e], ev_a_q_norm[e], ev_a_k_norm[e])
            ya = _dsa(aq, ak, av, iq, ik, iw)
            yb = _retention(bq, bk, bv, bg)
            h = _out_proj(h, flat(ya), flat(yb), ev_w_out[e].astype(BF16))
        else:
            o = layer // 2
            cq, ck, cv, dx = _odd_proj(seq(h), mix_norm[layer], od_w_in[o], od_c_q_norm[o],
                                       od_c_k_norm[o])
            yc = _swa(cq, ck, cv, od_c_sinks[o])
            yd = _pool(dx, od_d_mix[o], od_d_scale[o])
            h = _out_proj(h, flat(yc), flat(yd), od_w_out[o].astype(BF16))
        h = _ffn(h, ffn2_norm[layer], ffn2_w_in[layer].astype(BF16), ffn2_w_out[layer].astype(BF16))
    return h.reshape(b, lp, d)[:, PAD + N_META:]
```

```python
import functools

import jax
import jax.numpy as jnp
from jax import lax
from jax.experimental import pallas as pl
from jax.experimental.pallas import tpu as pltpu

F32 = jnp.float32
BF16 = jnp.bfloat16

D_MODEL = 1024
D_FF = 2816
CHUNK = 64
N_META = 16
PAD = CHUNK - N_META
HEAD_DIM = 64
LANES = 128
A_HEADS = 8
IDX_HEADS = 8
IDX_DIM = 64
TOP_K = 256
B_HEADS = 8
B_VDIM = 128
C_HEADS = 8
C_KV_HEADS = 2
C_WIN_CHUNKS = 2
D_WIDTH = 512
POOL_WINDOWS = (2, 4, 8, 16)
D_GROUP = 128
EPS = 1e-6
NEG = -1e30
BIG = 3e38
LOG2E = 1.4426950408889634

VMEM_LIMIT = 56 * 1024 * 1024
ROW_TILE = 512
FF_CHUNK = 256
PROJ_TILE = 704
KEY_TILE = 128
KEY_UNROLL = 4
DSA_BATCH = 2
MIX_BATCH = 2
RET_TILE = 528
BISECT_ITERS = 20

_NT = (((1,), (1,)), ((), ()))
_TN = (((0,), (0,)), ((), ()))


def _params(sem):
    return pltpu.CompilerParams(dimension_semantics=sem, vmem_limit_bytes=VMEM_LIMIT)


def _rms(x, gain=None):
    y = x * lax.rsqrt(jnp.mean(x * x, axis=-1, keepdims=True) + EPS)
    return y if gain is None else y * gain


def _resident(shape):
    return pl.BlockSpec(shape, lambda *_: (0,) * len(shape), pipeline_mode=pl.Buffered(1))


def _low_half():
    return lax.broadcasted_iota(jnp.int32, (1, LANES), 1) < HEAD_DIM


def _keep_head(x, half, low_half):
    return jnp.where(low_half if half == 0 else jnp.logical_not(low_half), x, jnp.zeros_like(x))


def _ffn_kernel(h_ref, g_ref, win_ref, wout_ref, o_ref, mix_refs=None):
    x = h_ref[...]
    if mix_refs is not None:
        ya_ref, yb_ref, wmix_ref = mix_refs
        na = ya_ref.shape[1]
        x = x + jnp.dot(ya_ref[...], wmix_ref[0:na, :], preferred_element_type=F32)
        x = x + jnp.dot(yb_ref[...], wmix_ref[na:, :], preferred_element_type=F32)
    xn = _rms(x, g_ref[...]).astype(BF16)
    acc = jnp.zeros(x.shape, F32)
    for c in range(D_FF // FF_CHUNK):
        lo = c * FF_CHUNK
        g = jnp.dot(xn, win_ref[:, lo:lo + FF_CHUNK], preferred_element_type=F32)
        u = jnp.dot(xn, win_ref[:, D_FF + lo:D_FF + lo + FF_CHUNK], preferred_element_type=F32)
        a = (jax.nn.silu(g) * u).astype(BF16)
        acc = acc + jnp.dot(a, wout_ref[lo:lo + FF_CHUNK, :], preferred_element_type=F32)
    o_ref[...] = x + 0.5 * acc


def _mix_ffn_kernel(h_ref, ya_ref, yb_ref, wmix_ref, g_ref, win_ref, wout_ref, o_ref):
    _ffn_kernel(h_ref, g_ref, win_ref, wout_ref, o_ref, (ya_ref, yb_ref, wmix_ref))


def _layer_weights(shape, layer):
    return pl.BlockSpec((None,) + shape, lambda *_: (layer, 0, 0), pipeline_mode=pl.Buffered(1))


def _ffn_weight_specs(layer):
    return [_resident((1, D_MODEL)), _layer_weights((D_MODEL, 2 * D_FF), layer),
            _layer_weights((D_FF, D_MODEL), layer)]


def _frame_rows(b, lp, s, n):
    return pl.BlockSpec(
        (pl.Element(ROW_TILE), pl.Element(n)),
        lambda bi, t: (pl.multiple_of(bi * lp + lp - s + ROW_TILE * t, CHUNK), 0))


def _ffn(h, gain, w_in, w_out, layer, frames_out=None, mix=None):
    if frames_out is None:
        rows = h.shape[0]
        grid = (rows // ROW_TILE,)
        row_spec = lambda n: pl.BlockSpec((ROW_TILE, n), lambda i: (i, 0))
        out_spec = row_spec(D_MODEL)
        out_shape = jax.ShapeDtypeStruct((rows, D_MODEL), F32)
        sem = ("parallel",)
    else:
        b, s = frames_out
        grid = (b, s // ROW_TILE)
        row_spec = functools.partial(_frame_rows, b, h.shape[0] // b, s)
        out_spec = pl.BlockSpec((None, ROW_TILE, D_MODEL), lambda bi, t: (bi, t, 0))
        out_shape = jax.ShapeDtypeStruct((b, s, D_MODEL), F32)
        sem = ("parallel", "parallel")
    if mix is None:
        body, rows_in, specs = _ffn_kernel, [h], [row_spec(D_MODEL)]
    else:
        ya, yb, w_mix = mix
        body, rows_in = _mix_ffn_kernel, [h, ya, yb, w_mix]
        specs = [row_spec(D_MODEL), row_spec(ya.shape[1]), row_spec(yb.shape[1]),
                 _resident(w_mix.shape)]
    return pl.pallas_call(
        body,
        grid=grid,
        in_specs=specs + _ffn_weight_specs(layer),
        out_specs=out_spec,
        out_shape=out_shape,
        compiler_params=_params(sem),
        name="ffn",
    )(*rows_in, gain.reshape(1, D_MODEL), w_in, w_out)


def _ffn_meta_kernel(c_ref, g_ref, win_ref, wout_ref, h_hbm_ref, o_ref, y_s):
    @pl.when(pl.program_id(0) == 0)
    def _():
        _ffn_kernel(c_ref, g_ref, win_ref, wout_ref, y_s)

    o_ref[...] = y_s[...]


def _ffn_first(x, meta_tokens, gain, w_in, w_out, layer):
    b, s, _ = x.shape
    lp = PAD + N_META + s
    tiles = s // ROW_TILE
    gain = gain.reshape(1, D_MODEL)
    h = pl.pallas_call(
        _ffn_kernel,
        grid=(b, tiles),
        in_specs=[pl.BlockSpec((ROW_TILE, D_MODEL), lambda bi, t: (bi * tiles + t, 0))]
        + _ffn_weight_specs(layer),
        out_specs=_frame_rows(b, lp, s, D_MODEL),
        out_shape=jax.ShapeDtypeStruct((b * lp, D_MODEL), F32),
        compiler_params=_params(("parallel", "parallel")),
        name="ffn_frames",
    )(x.reshape(b * s, D_MODEL), gain, w_in, w_out)
    chunk0 = jnp.concatenate([jnp.zeros((PAD, D_MODEL), x.dtype), meta_tokens.astype(x.dtype)], axis=0)
    return pl.pallas_call(
        _ffn_meta_kernel,
        grid=(b,),
        in_specs=[_resident((CHUNK, D_MODEL))] + _ffn_weight_specs(layer)
        + [pl.BlockSpec(memory_space=pl.ANY)],
        out_specs=pl.BlockSpec((CHUNK, D_MODEL), lambda bi: (bi * (lp // CHUNK), 0)),
        out_shape=jax.ShapeDtypeStruct((b * lp, D_MODEL), F32),
        scratch_shapes=[pltpu.VMEM((CHUNK, D_MODEL), F32)],
        input_output_aliases={4: 0},
        compiler_params=_params(("arbitrary",)),
        name="ffn_meta",
    )(chunk0, gain, w_in, w_out, h)


def _head_norm(y, seg_ref, gain):
    sq = (y * y).astype(BF16)
    ms = jnp.concatenate(
        [jnp.dot(sq[:, p * LANES:(p + 1) * LANES], seg_ref[...], preferred_element_type=F32)
         for p in range(y.shape[1] // LANES)], axis=-1)
    return y * lax.rsqrt(ms + EPS) * gain


def _rotate_half(x, first_half):
    n = x.shape[-1]
    half = HEAD_DIM // 2
    return jnp.where(first_half, -pltpu.roll(x, n - half, 1), pltpu.roll(x, half, 1))


def _even_proj_kernel(h_ref, g_ref, w_ref, seg_ref, qg_ref, kg_ref, cos_ref, sin_ref,
                      aq_ref, ak_ref, av_ref, iq_ref, ik_ref, iw_ref, bq_ref, bk_ref, bv_ref,
                      bg_ref):
    xn = _rms(h_ref[0], g_ref[...]).astype(BF16)
    off = [0]

    def cols(n):
        y = jnp.dot(xn, w_ref[:, off[0]:off[0] + n], preferred_element_type=F32)
        off[0] += n
        return y

    w = A_HEADS * HEAD_DIM
    aq_ref[0] = _head_norm(cols(w), seg_ref, qg_ref[...]).astype(BF16)
    ak_ref[0] = _head_norm(cols(w), seg_ref, kg_ref[...]).astype(BF16)
    av_ref[0] = cols(w).astype(BF16)
    iq_ref[0] = cols(IDX_HEADS * IDX_DIM).astype(BF16)
    ik_ref[0] = cols(LANES).astype(BF16)
    iw_ref[0] = cols(LANES) * (IDX_HEADS ** -0.5 * IDX_DIM ** -0.5)
    cos, sin = cos_ref[...], sin_ref[...]
    first_half = lax.broadcasted_iota(jnp.int32, (1, w), 1) % HEAD_DIM < HEAD_DIM // 2
    q = cols(w)
    bq_ref[0] = (q * cos + _rotate_half(q, first_half) * sin).astype(BF16)
    k = cols(w)
    bk_ref[0] = ((k * cos + _rotate_half(k, first_half) * sin) * HEAD_DIM ** -0.5).astype(BF16)
    for c in range(2):
        sl = slice(c * w, (c + 1) * w)
        bv_ref[0, :, sl] = cols(w).astype(BF16)
    for c in range(2):
        sl = slice(c * w, (c + 1) * w)
        bg_ref[0, :, sl] = cols(w)


def _odd_proj_kernel(h_ref, g_ref, w_ref, seg_ref, qg_ref, kg_ref, cq_ref, ck_ref, cv_ref, dx_ref):
    xn = _rms(h_ref[0], g_ref[...]).astype(BF16)
    off = [0]

    def cols(n):
        y = jnp.dot(xn, w_ref[:, off[0]:off[0] + n], preferred_element_type=F32)
        off[0] += n
        return y

    kv_w = 2 * C_KV_HEADS * HEAD_DIM
    cq_ref[0] = _head_norm(cols(C_HEADS * HEAD_DIM), seg_ref, qg_ref[...]).astype(BF16)
    ck_ref[0] = _head_norm(cols(kv_w), seg_ref, kg_ref[...]).astype(BF16)
    cv_ref[0] = cols(kv_w).astype(BF16)
    dx_ref[0] = cols(D_WIDTH)


def _seg_matrix():
    g = jnp.arange(LANES) // HEAD_DIM
    return jnp.where(g[:, None] == g[None, :], 1.0 / HEAD_DIM, 0.0).astype(BF16)


def _proj_call(body, name, h, gain, w, extra_in, extra_specs, outs):
    b, lp, _ = h.shape
    t = PROJ_TILE
    seq = lambda n: pl.BlockSpec((1, t, n), lambda bi, i: (bi, i, 0))
    return pl.pallas_call(
        body,
        grid=(b, lp // t),
        in_specs=[seq(D_MODEL), _resident((1, D_MODEL)), _resident(w.shape),
                  _resident((LANES, LANES))] + extra_specs,
        out_specs=[seq(n) for n, _ in outs],
        out_shape=[jax.ShapeDtypeStruct((b, lp, n), dt) for n, dt in outs],
        compiler_params=_params(("parallel", "parallel")),
        name=name,
    )(h, gain.reshape(1, D_MODEL), w, _seg_matrix(), *extra_in)


def _rotary_tables(lp):
    pos = (jnp.arange(lp) - PAD).astype(F32)
    inv = 1.0 / (10000.0 ** (jnp.arange(0, HEAD_DIM, 2, dtype=F32) / HEAD_DIM))
    ang = pos[:, None] * inv[None]
    return (jnp.tile(jnp.cos(ang), (1, 2 * B_HEADS)), jnp.tile(jnp.sin(ang), (1, 2 * B_HEADS)))


def _even_proj(h, gain, w, q_gain, k_gain):
    lp = h.shape[1]
    w = w.astype(BF16)
    a_w = 3 * A_HEADS * HEAD_DIM + IDX_HEADS * IDX_DIM
    ik = w[:, a_w:a_w + IDX_DIM]
    iw = jnp.pad(w[:, a_w + IDX_DIM:a_w + IDX_DIM + IDX_HEADS], ((0, 0), (0, LANES - IDX_HEADS)))
    w = jnp.concatenate([w[:, :a_w], ik, ik, iw, w[:, a_w + IDX_DIM + IDX_HEADS:]], axis=1)
    cos, sin = _rotary_tables(lp)
    w_heads = A_HEADS * HEAD_DIM
    qg = jnp.tile(q_gain, A_HEADS).reshape(1, w_heads) * (HEAD_DIM ** -0.5 * LOG2E)
    kg = jnp.tile(k_gain, A_HEADS).reshape(1, w_heads)
    table = pl.BlockSpec((PROJ_TILE, w_heads), lambda bi, i: (i, 0))
    outs = [(w_heads, BF16)] * 4 + [(LANES, BF16), (LANES, F32), (w_heads, BF16), (w_heads, BF16),
                                    (B_HEADS * B_VDIM, BF16), (B_HEADS * B_VDIM, F32)]
    return _proj_call(_even_proj_kernel, "even_in_proj", h, gain, w,
                      [qg, kg, cos, sin],
                      [_resident((1, w_heads)), _resident((1, w_heads)), table, table], outs)


def _odd_proj(h, gain, w, q_gain, k_gain):
    w = w.astype(BF16)
    q_w = C_HEADS * HEAD_DIM
    kv_w = C_KV_HEADS * HEAD_DIM
    dup = lambda m: jnp.concatenate(
        [m[:, g * HEAD_DIM:(g + 1) * HEAD_DIM] for g in range(C_KV_HEADS) for _ in range(2)], axis=1)
    w = jnp.concatenate([w[:, :q_w], dup(w[:, q_w:q_w + kv_w]), dup(w[:, q_w + kv_w:q_w + 2 * kv_w]),
                         w[:, q_w + 2 * kv_w:]], axis=1)
    qg = jnp.tile(q_gain, C_HEADS).reshape(1, q_w) * (HEAD_DIM ** -0.5 * LOG2E)
    kg = jnp.tile(k_gain, 2 * C_KV_HEADS).reshape(1, 2 * kv_w)
    outs = [(q_w, BF16), (2 * kv_w, BF16), (2 * kv_w, BF16), (D_WIDTH, F32)]
    return _proj_call(_odd_proj_kernel, "odd_in_proj", h, gain, w, [qg, kg],
                      [_resident((1, q_w)), _resident((1, 2 * kv_w))], outs)


def _fold(x):
    return x.reshape(x.shape[0] // 8, 8, x.shape[1]).sum(axis=0)


def _any(mask):
    return jnp.max(jnp.where(mask, 1.0, 0.0)) > 0.0


def _topk_threshold(sc_s, n_groups, lo, hi, search):
    kt = sc_s.shape[1]
    nq = sc_s.shape[2]
    key_in_tile = lax.broadcasted_iota(jnp.int32, (kt, 1), 0)

    def scan(fn, init):
        def body(g, acc):
            for u in range(KEY_UNROLL):
                c = g * KEY_UNROLL + u
                acc = fn(acc, sc_s[c], c)
            return acc
        return lax.fori_loop(0, n_groups, body, init)

    def count(pred):
        acc = scan(lambda a, s, c: a + _fold(jnp.where(pred(s, c), 1.0, 0.0)), jnp.zeros((8, nq), F32))
        return jnp.sum(acc, axis=0, keepdims=True)

    def step(lo, hi, mid):
        n_above = count(lambda s, c: s > mid)
        up = n_above >= TOP_K
        return jnp.where(up, mid, lo), jnp.where(up, hi, mid), up, n_above

    def coarse(_, c):
        lo, hi, n_lo = c
        lo, hi, up, n_above = step(lo, hi, 0.5 * (lo + hi))
        return lo, hi, jnp.where(up, n_above, n_lo)

    n_keys = kt * sc_s.shape[0]
    lo, hi, n_lo = lax.fori_loop(0, BISECT_ITERS, coarse, (lo, hi, jnp.full((1, nq), 2.0 * n_keys, F32)))

    def bounds(lo, hi):
        def fn(acc, s, c):
            t_lo, t_hi = acc
            t_lo = jnp.minimum(t_lo, jnp.min(jnp.where(s > lo, s, BIG), axis=0, keepdims=True))
            t_hi = jnp.maximum(t_hi, jnp.max(jnp.where(s <= hi, s, -BIG), axis=0, keepdims=True))
            return t_lo, t_hi
        return scan(fn, (jnp.full((1, nq), BIG, F32), jnp.full((1, nq), -BIG, F32)))

    def exact():
        def unfinished(c):
            _, _, t_lo, t_hi = c
            return _any((t_lo != t_hi) & search)

        def refine(c):
            lo, hi, t_lo, t_hi = c
            mid = 0.5 * (t_lo + t_hi)
            mid = jnp.where(mid >= t_hi, t_lo, mid)
            lo, hi = step(lo, hi, mid)[:2]
            return (lo, hi) + bounds(lo, hi)

        _, _, _, thr = lax.while_loop(unfinished, refine, (lo, hi) + bounds(lo, hi))
        need = TOP_K - count(lambda s, c: s > thr)

        def last_tie_key():
            def tie_step(_, c):
                lo_i, hi_i = c
                mid_i = (lo_i + hi_i) >> 1
                ok = count(lambda s, c: (s == thr) & (c * kt + key_in_tile <= mid_i)) >= need
                return jnp.where(ok, lo_i, mid_i), jnp.where(ok, mid_i, hi_i)

            init = (jnp.full((1, nq), -1, jnp.int32), jnp.full((1, nq), n_keys - 1, jnp.int32))
            return lax.fori_loop(0, n_keys.bit_length(), tie_step, init)[1]

        overflow = _any((count(lambda s, c: s == thr) > need) & search)
        return thr, lax.cond(overflow, last_tie_key, lambda: jnp.full((1, nq), n_keys, jnp.int32))

    settled = jnp.logical_not(_any((n_lo != TOP_K) & search))
    return lax.cond(settled, lambda: (lo, jnp.full((1, nq), -1, jnp.int32)), exact)


def _dsa_kernel(aq_ref, iq_ref, iw_ref, ak_ref, av_ref, ik_ref, o_ref, k_s, vt_s, ik_s, sc_s, acc_s):
    j = pl.program_id(1)
    nb = aq_ref.shape[0]
    lp = ak_ref.shape[1]
    kt = KEY_TILE
    n_full, tail = divmod(lp, kt)
    pairs = A_HEADS // 2
    rows = range(nb)

    @pl.when(j == 0)
    def _():
        def put(r, c, k, ik, v):
            k_s[r, c] = k
            ik_s[r, c] = ik
            v = v.astype(F32)
            for p in range(pairs):
                vt_s[r, c, p] = v[:, p * LANES:(p + 1) * LANES].T.astype(BF16)

        def copy_tile(c, carry):
            at = pl.ds(pl.multiple_of(c * kt, kt), kt)
            for r in rows:
                put(r, c, ak_ref[r, at, :], ik_ref[r, at, :], av_ref[r, at, :])
            return carry

        lax.fori_loop(0, n_full, copy_tile, 0)
        padded = lambda a: jnp.concatenate([a, jnp.zeros((kt - tail, a.shape[1]), a.dtype)], axis=0)
        at = slice(n_full * kt, lp)
        zeros = lambda ref: jnp.zeros(ref.shape[2:], BF16)
        for r in rows:
            if tail:
                put(r, n_full, padded(ak_ref[r, at, :]), padded(ik_ref[r, at, :]),
                    padded(av_ref[r, at, :]))
            for c in range(n_full + (tail > 0), k_s.shape[1]):
                put(r, c, zeros(k_s), zeros(ik_s), zeros(k_s))
        sc_s[...] = jnp.full(sc_s.shape, NEG, F32)

    low_half = _low_half()
    q_row = j * kt + lax.broadcasted_iota(jnp.int32, (kt, 1), 0)
    q_col = j * kt + lax.broadcasted_iota(jnp.int32, (1, kt), 1)
    in_range = q_row < lp

    def both_heads(x, p):
        x = x[:, p * LANES:(p + 1) * LANES]
        return jnp.concatenate([_keep_head(x, 0, low_half), _keep_head(x, 1, low_half)], axis=0)

    aq_p, iq_p, iw_t = [], [], []
    for r in rows:
        aq = jnp.where(in_range, aq_ref[r], jnp.zeros_like(aq_ref[r]))
        iq = jnp.where(in_range, iq_ref[r], jnp.zeros_like(iq_ref[r]))
        aq_p.append([both_heads(aq, p) for p in range(pairs)])
        iq_p.append([both_heads(iq, p) for p in range(pairs)])
        iw_t.append(jnp.where(in_range, iw_ref[r], 0.0).T)
    key_in_tile = lax.broadcasted_iota(jnp.int32, (kt, 1), 0)
    key_grid = lax.broadcasted_iota(jnp.int32, (kt, kt), 0)
    key_limit = jnp.minimum((q_col // CHUNK + 1) * CHUNK, lp)
    n_groups = (j + KEY_UNROLL) // KEY_UNROLL
    cols = lambda r: slice(r * kt, (r + 1) * kt)

    def score_tiles(g, carry):
        mn, mx = carry
        tiles = [g * KEY_UNROLL + u for u in range(KEY_UNROLL)]
        logits = [[[lax.dot_general(ik_s[r, c], iq_p[r][p], _NT, preferred_element_type=F32)
                    for p in range(pairs)] for r in rows] for c in tiles]
        mn, mx = list(mn), list(mx)
        for c, per_row in zip(tiles, logits):
            key = key_grid + c * kt
            admissible = (key >= jnp.where(c == 0, PAD, 0)) & (key < key_limit)
            for r, lg in zip(rows, per_row):
                sc = jnp.zeros((kt, kt), F32)
                for p in range(pairs):
                    for half in range(2):
                        h = 2 * p + half
                        sc = sc + iw_t[r][h:h + 1, :] * jnp.maximum(
                            lg[p][:, half * kt:(half + 1) * kt], 0.0)
                sc_s[c, :, cols(r)] = jnp.where(admissible, sc, NEG)
                mn[r] = jnp.minimum(mn[r], jnp.min(jnp.where(admissible, sc, BIG), axis=0, keepdims=True))
                mx[r] = jnp.maximum(mx[r], jnp.max(jnp.where(admissible, sc, NEG), axis=0, keepdims=True))
        return tuple(mn), tuple(mx)

    mn, mx = lax.fori_loop(0, n_groups, score_tiles,
                           (tuple(jnp.full((1, kt), BIG, F32) for _ in rows),
                            tuple(jnp.full((1, kt), NEG, F32) for _ in rows)))
    mn = jnp.concatenate(mn, axis=1)
    mx = jnp.concatenate(mx, axis=1)

    first_search_block = (TOP_K - N_META + CHUNK - 1) // CHUNK * CHUNK // kt

    def to_bias(keep):
        def body(g, carry):
            for u in range(KEY_UNROLL):
                c = g * KEY_UNROLL + u
                sc_s[c] = jnp.where(keep(sc_s[c], c), 0.0, NEG)
            return carry
        lax.fori_loop(0, n_groups, body, 0)

    @pl.when(j < first_search_block)
    def _():
        to_bias(lambda s, c: s > 0.5 * NEG)

    @pl.when(j >= first_search_block)
    def _():
        search = jnp.concatenate([q_col < lp] * nb, axis=1)
        thr, limit = _topk_threshold(sc_s, n_groups, mn - (jnp.abs(mn) + 1.0), mx, search)
        to_bias(lambda s, c: (s > thr) | ((s == thr) & (c * kt + key_in_tile <= limit)))

    acc_s[...] = jnp.zeros(acc_s.shape, F32)
    chains = [(r, p) for r in rows for p in range(pairs)]

    def attend(g, carry):
        m, l = carry
        c0 = g * KEY_UNROLL
        s = []
        for r in rows:
            bias = jnp.concatenate([sc_s[c0 + u, :, cols(r)] for u in range(KEY_UNROLL)], axis=0)
            bias = jnp.concatenate([bias, bias], axis=1)
            for p in range(pairs):
                k = jnp.concatenate([k_s[r, c0 + u, :, p * LANES:(p + 1) * LANES]
                                     for u in range(KEY_UNROLL)], axis=0)
                s.append(lax.dot_general(k, aq_p[r][p], _NT, preferred_element_type=F32) + bias)
        new_m, new_l = [], []
        for i, (r, p) in enumerate(chains):
            m_i = jnp.maximum(m[i], jnp.max(s[i], axis=0, keepdims=True))
            alpha = jnp.exp2(m[i] - m_i)
            pr = jnp.exp2(s[i] - m_i)
            new_l.append(alpha * l[i] + jnp.sum(pr, axis=0, keepdims=True))
            vt = jnp.concatenate([vt_s[r, c0 + u, p] for u in range(KEY_UNROLL)], axis=1)
            acc_s[r, p] = alpha * acc_s[r, p] + jnp.dot(vt, pr.astype(BF16), preferred_element_type=F32)
            new_m.append(m_i)
        return tuple(new_m), tuple(new_l)

    init = (tuple(jnp.full((1, 2 * kt), -BIG, F32) for _ in chains),
            tuple(jnp.zeros((1, 2 * kt), F32) for _ in chains))
    _, l = lax.fori_loop(0, n_groups, attend, init)

    v_row_low = lax.broadcasted_iota(jnp.int32, (LANES, 1), 0) < HEAD_DIM
    for i, (r, p) in enumerate(chains):
        out_t = acc_s[r, p] / l[i]
        out_t = jnp.where(v_row_low, out_t[:, 0:kt], out_t[:, kt:2 * kt])
        o_ref[r, :, p * LANES:(p + 1) * LANES] = jnp.where(q_row >= PAD, out_t.T, 0.0).astype(BF16)


def _dsa(aq, ak, av, iq, ik, iw):
    b, lp, width = aq.shape
    kt = KEY_TILE
    nb = DSA_BATCH
    n_tiles = -(-lp // kt)
    n_scan = -(-n_tiles // KEY_UNROLL) * KEY_UNROLL
    q_spec = lambda n: pl.BlockSpec((nb, kt, n), lambda bi, j: (bi, j, 0))
    full_spec = lambda n: pl.BlockSpec((nb, lp, n), lambda bi, j: (bi, 0, 0))
    return pl.pallas_call(
        _dsa_kernel,
        grid=(b // nb, n_tiles),
        in_specs=[q_spec(width), q_spec(width), q_spec(LANES), full_spec(width), full_spec(width),
                  full_spec(LANES)],
        out_specs=q_spec(width),
        out_shape=jax.ShapeDtypeStruct((b, lp, width), BF16),
        scratch_shapes=[
            pltpu.VMEM((nb, n_scan, kt, width), BF16),
            pltpu.VMEM((nb, n_scan, A_HEADS // 2, LANES, kt), BF16),
            pltpu.VMEM((nb, n_scan, kt, LANES), BF16),
            pltpu.VMEM((n_scan, kt, nb * kt), F32),
            pltpu.VMEM((nb, A_HEADS // 2, LANES, 2 * kt), F32),
        ],
        compiler_params=_params(("parallel", "arbitrary")),
        name="dsa_attention",
    )(aq, iq, iw, ak, av, ik)


def _retention_kernel(q_ref, k_ref, v_ref, g_ref, dmat_ref, qdec_ref, kdec_ref, cdec_ref, o_ref,
                      state):
    i = pl.program_id(1)

    @pl.when(i == 0)
    def _():
        state[...] = jnp.zeros(state.shape, F32)

    low_half = _low_half()
    for pair in range(B_HEADS // 2):
        sl = slice(pair * LANES, (pair + 1) * LANES)
        for r in range(q_ref.shape[0]):
            q_pair = q_ref[r, :, sl]
            k_pair = k_ref[r, :, sl]
            s_old = state[r, pair]
            s_new = s_old * cdec_ref[pair]
            for half in range(2):
                h = 2 * pair + half
                vsl = slice(h * B_VDIM, (h + 1) * B_VDIM)
                v = v_ref[r, :, vsl]
                q = _keep_head(q_pair, half, low_half)
                inner = lax.dot_general(q, k_pair, _NT, preferred_element_type=F32) * dmat_ref[h]
                y = jnp.dot(inner.astype(BF16), v, preferred_element_type=F32)
                y = y + jnp.dot(q, s_old.astype(BF16), preferred_element_type=F32) * qdec_ref[h]
                kd = _keep_head((k_pair.astype(F32) * kdec_ref[h]).astype(BF16), half, low_half)
                s_new = s_new + lax.dot_general(kd, v, _TN, preferred_element_type=F32)
                o_ref[r, :, vsl] = (_rms(y) * jax.nn.silu(g_ref[r, :, vsl])).astype(BF16)
            state[r, pair] = s_new


def _retention(q, k, v, gate):
    b, lp, _ = q.shape
    t = RET_TILE
    log_g = jnp.log(1.0 - 2.0 ** (-5.0 - jnp.arange(B_HEADS, dtype=F32)))
    idx = jnp.arange(t, dtype=F32)
    rel = idx[:, None] - idx[None, :]
    dmat = jnp.where(rel >= 0, jnp.exp(jnp.maximum(rel, 0.0)[None] * log_g[:, None, None]), 0.0)
    qdec = jnp.exp((idx + 1.0)[None, :, None] * log_g[:, None, None])
    kdec = jnp.exp((t - 1.0 - idx)[None, :, None] * log_g[:, None, None])
    cdec = jnp.repeat(jnp.exp(t * log_g), HEAD_DIM).reshape(B_HEADS // 2, LANES, 1)

    qk_w, v_w = B_HEADS * HEAD_DIM, B_HEADS * B_VDIM
    nb = MIX_BATCH
    seq = lambda n: pl.BlockSpec((nb, t, n), lambda bi, i: (bi, i, 0))
    return pl.pallas_call(
        _retention_kernel,
        grid=(b // nb, lp // t),
        in_specs=[seq(qk_w), seq(qk_w), seq(v_w), seq(v_w),
                  _resident((B_HEADS, t, t)), _resident((B_HEADS, t, 1)),
                  _resident((B_HEADS, t, 1)), _resident((B_HEADS // 2, LANES, 1))],
        out_specs=seq(v_w),
        out_shape=jax.ShapeDtypeStruct((b, lp, v_w), BF16),
        scratch_shapes=[pltpu.VMEM((nb, B_HEADS // 2, LANES, B_VDIM), F32)],
        compiler_params=_params(("parallel", "arbitrary")),
        name="retention",
    )(q, k, v, gate, dmat, qdec, kdec, cdec)


def _swa_kernel(q_ref, k_ref, v_ref, sink_ref, o_ref):
    j = pl.program_id(1)
    nb, tq = q_ref.shape[0], q_ref.shape[1]
    lp = k_ref.shape[1]
    win = tq + C_WIN_CHUNKS * CHUNK
    rep = C_HEADS // C_KV_HEADS
    low_half = _low_half()

    start = pl.multiple_of(jnp.clip(j * tq - C_WIN_CHUNKS * CHUNK, 0, lp - win), CHUNK)
    q_row = j * tq + lax.broadcasted_iota(jnp.int32, (tq, 1), 0)
    q_col = j * tq + lax.broadcasted_iota(jnp.int32, (1, tq), 1)
    q_chunk = jnp.concatenate([q_col // CHUNK] * rep, axis=1)
    key = lax.broadcasted_iota(jnp.int32, (CHUNK + win, 1), 0)
    k_chunk = start // CHUNK + (key - CHUNK) // CHUNK
    in_window = ((key >= CHUNK) & (k_chunk >= 1) & (k_chunk <= q_chunk)
                 & (k_chunk >= q_chunk - C_WIN_CHUNKS))
    valid = in_window | ((key < CHUNK) & (key >= PAD))
    v_row_low = lax.broadcasted_iota(jnp.int32, (LANES, 1), 0) < HEAD_DIM

    for r in range(nb):
        q = jnp.where(q_row < lp, q_ref[r], jnp.zeros_like(q_ref[r]))
        for g in range(C_KV_HEADS):
            sl = slice(g * LANES, (g + 1) * LANES)
            keys = jnp.concatenate([k_ref[r, 0:CHUNK, sl], k_ref[r, pl.ds(start, win), sl]], axis=0)
            vals = jnp.concatenate([v_ref[r, 0:CHUNK, sl], v_ref[r, pl.ds(start, win), sl]], axis=0)
            heads = range(g * rep, (g + 1) * rep)
            qs = jnp.concatenate(
                [_keep_head(q[:, (h // 2) * LANES:(h // 2 + 1) * LANES], h % 2, low_half)
                 for h in heads], axis=0)
            sink = jnp.concatenate(
                [jnp.broadcast_to(sink_ref[0:1, h:h + 1] * LOG2E, (1, tq)) for h in heads], axis=1)
            s = jnp.where(valid, lax.dot_general(keys, qs, _NT, preferred_element_type=F32), NEG)
            m = jnp.maximum(jnp.max(s, axis=0, keepdims=True), sink)
            p = jnp.exp2(s - m)
            denom = jnp.sum(p, axis=0, keepdims=True) + jnp.exp2(sink - m)
            y_t = lax.dot_general(vals, p.astype(BF16), _TN, preferred_element_type=F32) / denom
            for u in range(rep // 2):
                out_t = jnp.where(v_row_low, y_t[:, 2 * u * tq:(2 * u + 1) * tq],
                                  y_t[:, (2 * u + 1) * tq:(2 * u + 2) * tq])
                pair = g * rep // 2 + u
                o_ref[r, :, pair * LANES:(pair + 1) * LANES] = jnp.where(
                    q_row >= PAD, out_t.T, 0.0).astype(BF16)


def _swa(q, k, v, sinks):
    b, lp, width = q.shape
    tq = KEY_TILE
    nb = MIX_BATCH
    kv = pl.BlockSpec((nb, lp, k.shape[2]), lambda bi, j: (bi, 0, 0))
    return pl.pallas_call(
        _swa_kernel,
        grid=(b // nb, -(-lp // tq)),
        in_specs=[pl.BlockSpec((nb, tq, width), lambda bi, j: (bi, j, 0)), kv, kv,
                  _resident((1, C_HEADS))],
        out_specs=pl.BlockSpec((nb, tq, width), lambda bi, j: (bi, j, 0)),
        out_shape=jax.ShapeDtypeStruct((b, lp, width), BF16),
        compiler_params=_params(("parallel", "parallel")),
        name="swa_sinks",
    )(q, k, v, sinks.reshape(1, C_HEADS))


def _pool_kernel(x_ref, mix_ref, scale_ref, o_ref):
    lp = x_ref.shape[1]
    row = lax.broadcasted_iota(jnp.int32, (lp, 1), 0)
    t = row - PAD
    outs = []
    for gi, w in enumerate(POOL_WINDOWS):
        x = x_ref[0, :, gi * D_GROUP:(gi + 1) * D_GROUP]
        s, shift = x, 1
        while shift < w:
            s = s + jnp.where(row >= shift, pltpu.roll(s, shift, 0), 0.0)
            shift *= 2
        count = jnp.maximum(jnp.minimum(t + 1, w), 1).astype(F32)
        y = (s / count - x).astype(BF16)
        outs.append(jnp.dot(y, mix_ref[gi].astype(BF16), preferred_element_type=F32))
    out = jnp.concatenate(outs, axis=-1) * scale_ref[...]
    o_ref[0] = jnp.where(row >= PAD, out, 0.0).astype(BF16)


def _pool(x, mix, scale):
    b, lp, width = x.shape
    n_groups = len(POOL_WINDOWS)
    return pl.pallas_call(
        _pool_kernel,
        grid=(b,),
        in_specs=[pl.BlockSpec((1, lp, width), lambda bi: (bi, 0, 0)),
                  _resident((n_groups, D_GROUP, D_GROUP)), _resident((1, width))],
        out_specs=pl.BlockSpec((1, lp, width), lambda bi: (bi, 0, 0)),
        out_shape=jax.ShapeDtypeStruct((b, lp, width), BF16),
        compiler_params=_params(("parallel",)),
        name="pool_mixer",
    )(x, mix, scale.reshape(1, width))


def kernel(x, meta_tokens, ffn1_norm, ffn1_w_in, ffn1_w_out, mix_norm, ffn2_norm, ffn2_w_in,
           ffn2_w_out, ev_w_in, ev_a_q_norm, ev_a_k_norm, ev_w_out, od_w_in, od_c_q_norm,
           od_c_k_norm, od_c_sinks, od_d_mix, od_d_scale, od_w_out):
    b, s, d = x.shape
    lp = PAD + N_META + s
    assert d == D_MODEL and (b * lp) % ROW_TILE == 0 and s % ROW_TILE == 0
    assert lp % PROJ_TILE == 0 and lp % RET_TILE == 0
    assert min(TOP_K, s // 4) == TOP_K and b % DSA_BATCH == 0 and b % MIX_BATCH == 0
    depth = ffn1_norm.shape[0]
    seq = lambda a: a.reshape(b, lp, a.shape[-1])
    flat = lambda a: a.reshape(b * lp, a.shape[-1])

    ffn1_w_in, ffn1_w_out, ffn2_w_in, ffn2_w_out = (
        w.astype(BF16) for w in (ffn1_w_in, ffn1_w_out, ffn2_w_in, ffn2_w_out))
    for layer in range(depth):
        if layer == 0:
            h = _ffn_first(x, meta_tokens, ffn1_norm[0], ffn1_w_in, ffn1_w_out, 0)
        else:
            h = _ffn(h, ffn1_norm[layer], ffn1_w_in, ffn1_w_out, layer)
        if layer % 2 == 0:
            e = layer // 2
            aq, ak, av, iq, ik, iw, bq, bk, bv, bg = _even_proj(
                seq(h), mix_norm[layer], ev_w_in[e], ev_a_q_norm[e], ev_a_k_norm[e])
            ya = _dsa(aq, ak, av, iq, ik, iw)
            yb = _retention(bq, bk, bv, bg)
            mix = (flat(ya), flat(yb), ev_w_out[e].astype(BF16))
        else:
            o = layer // 2
            cq, ck, cv, dx = _odd_proj(seq(h), mix_norm[layer], od_w_in[o], od_c_q_norm[o],
                                       od_c_k_norm[o])
            yc = _swa(cq, ck, cv, od_c_sinks[o])
            yd = _pool(dx, od_d_mix[o], od_d_scale[o])
            mix = (flat(yc), flat(yd), od_w_out[o].astype(BF16))
        last = layer == depth - 1
        h = _ffn(h, ffn2_norm[layer], ffn2_w_in, ffn2_w_out, layer, (b, s) if last else None, mix)
    return h
```

```python
import functools

import jax
import jax.numpy as jnp
from jax import lax
from jax.experimental import pallas as pl
from jax.experimental.pallas import tpu as pltpu

F32 = jnp.float32
BF16 = jnp.bfloat16

D_MODEL = 1024
D_FF = 2816
CHUNK = 64
N_META = 16
PAD = CHUNK - N_META
HEAD_DIM = 64
LANES = 128
A_HEADS = 8
IDX_HEADS = 8
IDX_DIM = 64
TOP_K = 256
B_HEADS = 8
B_VDIM = 128
C_HEADS = 8
C_KV_HEADS = 2
C_WIN_CHUNKS = 2
D_WIDTH = 512
POOL_WINDOWS = (2, 4, 8, 16)
D_GROUP = 128
EPS = 1e-6
NEG = -1e30
BIG = 3e38
LOG2E = 1.4426950408889634

VMEM_LIMIT = 56 * 1024 * 1024
ROW_TILE = 512
FF_CHUNK = 256
PROJ_TILE = 704
KEY_TILE = 128
KEY_UNROLL = 4
DSA_BATCH = 2
MIX_BATCH = 2
RET_TILE = 528
BISECT_ITERS = 20

_NT = (((1,), (1,)), ((), ()))
_TN = (((0,), (0,)), ((), ()))


def _params(sem):
    return pltpu.CompilerParams(dimension_semantics=sem, vmem_limit_bytes=VMEM_LIMIT)


def _rms(x, gain=None):
    y = x * lax.rsqrt(jnp.mean(x * x, axis=-1, keepdims=True) + EPS)
    return y if gain is None else y * gain


def _resident(shape):
    return pl.BlockSpec(shape, lambda *_: (0,) * len(shape), pipeline_mode=pl.Buffered(1))


def _low_half():
    return lax.broadcasted_iota(jnp.int32, (1, LANES), 1) < HEAD_DIM


def _keep_head(x, half, low_half):
    return jnp.where(low_half if half == 0 else jnp.logical_not(low_half), x, jnp.zeros_like(x))


def _ffn_kernel(h_ref, g_ref, win_ref, wout_ref, o_ref, mix_refs=None):
    x = h_ref[...]
    if mix_refs is not None:
        ya_ref, yb_ref, wmix_ref = mix_refs
        na = ya_ref.shape[1]
        x = x + jnp.dot(ya_ref[...], wmix_ref[0:na, :], preferred_element_type=F32)
        x = x + jnp.dot(yb_ref[...], wmix_ref[na:, :], preferred_element_type=F32)
    xn = _rms(x, g_ref[...]).astype(BF16)
    acc = jnp.zeros(x.shape, F32)
    for c in range(D_FF // FF_CHUNK):
        lo = c * FF_CHUNK
        g = jnp.dot(xn, win_ref[:, lo:lo + FF_CHUNK], preferred_element_type=F32)
        u = jnp.dot(xn, win_ref[:, D_FF + lo:D_FF + lo + FF_CHUNK], preferred_element_type=F32)
        a = (jax.nn.silu(g) * u).astype(BF16)
        acc = acc + jnp.dot(a, wout_ref[lo:lo + FF_CHUNK, :], preferred_element_type=F32)
    o_ref[...] = x + 0.5 * acc


def _mix_ffn_kernel(h_ref, ya_ref, yb_ref, wmix_ref, g_ref, win_ref, wout_ref, o_ref):
    _ffn_kernel(h_ref, g_ref, win_ref, wout_ref, o_ref, (ya_ref, yb_ref, wmix_ref))


def _layer_weights(shape, layer):
    return pl.BlockSpec((None,) + shape, lambda *_: (layer, 0, 0), pipeline_mode=pl.Buffered(1))


def _ffn_weight_specs(layer):
    return [_resident((1, D_MODEL)), _layer_weights((D_MODEL, 2 * D_FF), layer),
            _layer_weights((D_FF, D_MODEL), layer)]


def _frame_rows(b, lp, s, n):
    return pl.BlockSpec(
        (pl.Element(ROW_TILE), pl.Element(n)),
        lambda bi, t: (pl.multiple_of(bi * lp + lp - s + ROW_TILE * t, CHUNK), 0))


def _ffn(h, gain, w_in, w_out, layer, frames_out=None, mix=None):
    if frames_out is None:
        rows = h.shape[0]
        grid = (rows // ROW_TILE,)
        row_spec = lambda n: pl.BlockSpec((ROW_TILE, n), lambda i: (i, 0))
        out_spec = row_spec(D_MODEL)
        out_shape = jax.ShapeDtypeStruct((rows, D_MODEL), F32)
        sem = ("parallel",)
    else:
        b, s = frames_out
        grid = (b, s // ROW_TILE)
        row_spec = functools.partial(_frame_rows, b, h.shape[0] // b, s)
        out_spec = pl.BlockSpec((None, ROW_TILE, D_MODEL), lambda bi, t: (bi, t, 0))
        out_shape = jax.ShapeDtypeStruct((b, s, D_MODEL), F32)
        sem = ("parallel", "parallel")
    if mix is None:
        body, rows_in, specs = _ffn_kernel, [h], [row_spec(D_MODEL)]
    else:
        ya, yb, w_mix = mix
        body, rows_in = _mix_ffn_kernel, [h, ya, yb, w_mix]
        specs = [row_spec(D_MODEL), row_spec(ya.shape[1]), row_spec(yb.shape[1]),
                 _resident(w_mix.shape)]
    return pl.pallas_call(
        body,
        grid=grid,
        in_specs=specs + _ffn_weight_specs(layer),
        out_specs=out_spec,
        out_shape=out_shape,
        compiler_params=_params(sem),
        name="ffn",
    )(*rows_in, gain.reshape(1, D_MODEL), w_in, w_out)


def _ffn_meta_kernel(c_ref, g_ref, win_ref, wout_ref, h_hbm_ref, o_ref, y_s):
    @pl.when(pl.program_id(0) == 0)
    def _():
        _ffn_kernel(c_ref, g_ref, win_ref, wout_ref, y_s)

    o_ref[...] = y_s[...]


def _ffn_first(x, meta_tokens, gain, w_in, w_out, layer):
    b, s, _ = x.shape
    lp = PAD + N_META + s
    tiles = s // ROW_TILE
    gain = gain.reshape(1, D_MODEL)
    h = pl.pallas_call(
        _ffn_kernel,
        grid=(b, tiles),
        in_specs=[pl.BlockSpec((ROW_TILE, D_MODEL), lambda bi, t: (bi * tiles + t, 0))]
        + _ffn_weight_specs(layer),
        out_specs=_frame_rows(b, lp, s, D_MODEL),
        out_shape=jax.ShapeDtypeStruct((b * lp, D_MODEL), F32),
        compiler_params=_params(("parallel", "parallel")),
        name="ffn_frames",
    )(x.reshape(b * s, D_MODEL), gain, w_in, w_out)
    chunk0 = jnp.concatenate([jnp.zeros((PAD, D_MODEL), x.dtype), meta_tokens.astype(x.dtype)], axis=0)
    return pl.pallas_call(
        _ffn_meta_kernel,
        grid=(b,),
        in_specs=[_resident((CHUNK, D_MODEL))] + _ffn_weight_specs(layer)
        + [pl.BlockSpec(memory_space=pl.ANY)],
        out_specs=pl.BlockSpec((CHUNK, D_MODEL), lambda bi: (bi * (lp // CHUNK), 0)),
        out_shape=jax.ShapeDtypeStruct((b * lp, D_MODEL), F32),
        scratch_shapes=[pltpu.VMEM((CHUNK, D_MODEL), F32)],
        input_output_aliases={4: 0},
        compiler_params=_params(("arbitrary",)),
        name="ffn_meta",
    )(chunk0, gain, w_in, w_out, h)


def _head_norm(y, seg_ref, gain):
    sq = (y * y).astype(BF16)
    ms = jnp.concatenate(
        [jnp.dot(sq[:, p * LANES:(p + 1) * LANES], seg_ref[...], preferred_element_type=F32)
         for p in range(y.shape[1] // LANES)], axis=-1)
    return y * lax.rsqrt(ms + EPS) * gain


def _rotate_half(x, first_half):
    n = x.shape[-1]
    half = HEAD_DIM // 2
    return jnp.where(first_half, -pltpu.roll(x, n - half, 1), pltpu.roll(x, half, 1))


def _even_proj_kernel(h_ref, g_ref, w_ref, seg_ref, qg_ref, kg_ref, cos_ref, sin_ref,
                      aq_ref, ak_ref, av_ref, iq_ref, ik_ref, iw_ref, bq_ref, bk_ref, bv_ref,
                      bg_ref):
    xn = _rms(h_ref[0], g_ref[...]).astype(BF16)
    off = [0]

    def cols(n):
        y = jnp.dot(xn, w_ref[:, off[0]:off[0] + n], preferred_element_type=F32)
        off[0] += n
        return y

    w = A_HEADS * HEAD_DIM
    aq_ref[0] = _head_norm(cols(w), seg_ref, qg_ref[...]).astype(BF16)
    ak_ref[0] = _head_norm(cols(w), seg_ref, kg_ref[...]).astype(BF16)
    av_ref[0] = cols(w).astype(BF16)
    iq_ref[0] = cols(IDX_HEADS * IDX_DIM).astype(BF16)
    ik_ref[0] = cols(LANES).astype(BF16)
    iw_ref[0] = cols(LANES) * (IDX_HEADS ** -0.5 * IDX_DIM ** -0.5)
    cos, sin = cos_ref[...], sin_ref[...]
    first_half = lax.broadcasted_iota(jnp.int32, (1, w), 1) % HEAD_DIM < HEAD_DIM // 2
    q = cols(w)
    bq_ref[0] = (q * cos + _rotate_half(q, first_half) * sin).astype(BF16)
    k = cols(w)
    bk_ref[0] = ((k * cos + _rotate_half(k, first_half) * sin) * HEAD_DIM ** -0.5).astype(BF16)
    for c in range(2):
        sl = slice(c * w, (c + 1) * w)
        bv_ref[0, :, sl] = cols(w).astype(BF16)
    for c in range(2):
        sl = slice(c * w, (c + 1) * w)
        bg_ref[0, :, sl] = cols(w)


def _odd_proj_kernel(h_ref, g_ref, w_ref, seg_ref, qg_ref, kg_ref, cq_ref, ck_ref, cv_ref, dx_ref):
    xn = _rms(h_ref[0], g_ref[...]).astype(BF16)
    off = [0]

    def cols(n):
        y = jnp.dot(xn, w_ref[:, off[0]:off[0] + n], preferred_element_type=F32)
        off[0] += n
        return y

    kv_w = 2 * C_KV_HEADS * HEAD_DIM
    cq_ref[0] = _head_norm(cols(C_HEADS * HEAD_DIM), seg_ref, qg_ref[...]).astype(BF16)
    ck_ref[0] = _head_norm(cols(kv_w), seg_ref, kg_ref[...]).astype(BF16)
    cv_ref[0] = cols(kv_w).astype(BF16)
    dx_ref[0] = cols(D_WIDTH)


def _seg_matrix():
    g = jnp.arange(LANES) // HEAD_DIM
    return jnp.where(g[:, None] == g[None, :], 1.0 / HEAD_DIM, 0.0).astype(BF16)


def _proj_call(body, name, h, gain, w, extra_in, extra_specs, outs):
    b, lp, _ = h.shape
    t = PROJ_TILE
    seq = lambda n: pl.BlockSpec((1, t, n), lambda bi, i: (bi, i, 0))
    return pl.pallas_call(
        body,
        grid=(b, lp // t),
        in_specs=[seq(D_MODEL), _resident((1, D_MODEL)), _resident(w.shape),
                  _resident((LANES, LANES))] + extra_specs,
        out_specs=[seq(n) for n, _ in outs],
        out_shape=[jax.ShapeDtypeStruct((b, lp, n), dt) for n, dt in outs],
        compiler_params=_params(("parallel", "parallel")),
        name=name,
    )(h, gain.reshape(1, D_MODEL), w, _seg_matrix(), *extra_in)


def _rotary_tables(lp):
    pos = (jnp.arange(lp) - PAD).astype(F32)
    inv = 1.0 / (10000.0 ** (jnp.arange(0, HEAD_DIM, 2, dtype=F32) / HEAD_DIM))
    ang = pos[:, None] * inv[None]
    return (jnp.tile(jnp.cos(ang), (1, 2 * B_HEADS)), jnp.tile(jnp.sin(ang), (1, 2 * B_HEADS)))


def _even_proj(h, gain, w, q_gain, k_gain):
    lp = h.shape[1]
    w = w.astype(BF16)
    a_w = 3 * A_HEADS * HEAD_DIM + IDX_HEADS * IDX_DIM
    ik = w[:, a_w:a_w + IDX_DIM]
    iw = jnp.pad(w[:, a_w + IDX_DIM:a_w + IDX_DIM + IDX_HEADS], ((0, 0), (0, LANES - IDX_HEADS)))
    w = jnp.concatenate([w[:, :a_w], ik, ik, iw, w[:, a_w + IDX_DIM + IDX_HEADS:]], axis=1)
    cos, sin = _rotary_tables(lp)
    w_heads = A_HEADS * HEAD_DIM
    qg = jnp.tile(q_gain, A_HEADS).reshape(1, w_heads) * (HEAD_DIM ** -0.5 * LOG2E)
    kg = jnp.tile(k_gain, A_HEADS).reshape(1, w_heads)
    table = pl.BlockSpec((PROJ_TILE, w_heads), lambda bi, i: (i, 0))
    outs = [(w_heads, BF16)] * 4 + [(LANES, BF16), (LANES, F32), (w_heads, BF16), (w_heads, BF16),
                                    (B_HEADS * B_VDIM, BF16), (B_HEADS * B_VDIM, F32)]
    return _proj_call(_even_proj_kernel, "even_in_proj", h, gain, w,
                      [qg, kg, cos, sin],
                      [_resident((1, w_heads)), _resident((1, w_heads)), table, table], outs)


def _odd_proj(h, gain, w, q_gain, k_gain):
    w = w.astype(BF16)
    q_w = C_HEADS * HEAD_DIM
    kv_w = C_KV_HEADS * HEAD_DIM
    dup = lambda m: jnp.concatenate(
        [m[:, g * HEAD_DIM:(g + 1) * HEAD_DIM] for g in range(C_KV_HEADS) for _ in range(2)], axis=1)
    w = jnp.concatenate([w[:, :q_w], dup(w[:, q_w:q_w + kv_w]), dup(w[:, q_w + kv_w:q_w + 2 * kv_w]),
                         w[:, q_w + 2 * kv_w:]], axis=1)
    qg = jnp.tile(q_gain, C_HEADS).reshape(1, q_w) * (HEAD_DIM ** -0.5 * LOG2E)
    kg = jnp.tile(k_gain, 2 * C_KV_HEADS).reshape(1, 2 * kv_w)
    outs = [(q_w, BF16), (2 * kv_w, BF16), (2 * kv_w, BF16), (D_WIDTH, F32)]
    return _proj_call(_odd_proj_kernel, "odd_in_proj", h, gain, w, [qg, kg],
                      [_resident((1, q_w)), _resident((1, 2 * kv_w))], outs)


def _fold(x):
    return x.reshape(x.shape[0] // 8, 8, x.shape[1]).sum(axis=0)


def _any(mask):
    return jnp.max(jnp.where(mask, 1.0, 0.0)) > 0.0


def _topk_threshold(sc_s, n_groups, lo, hi, search):
    nq = sc_s.shape[2]

    def scan(fn, init):
        def body(g, acc):
            for u in range(KEY_UNROLL):
                c = g * KEY_UNROLL + u
                acc = fn(acc, sc_s[c], c)
            return acc
        return lax.fori_loop(0, n_groups, body, init)

    def count(pred):
        acc = scan(lambda a, s, c: a + _fold(jnp.where(pred(s, c), 1.0, 0.0)), jnp.zeros((8, nq), F32))
        return jnp.sum(acc, axis=0, keepdims=True)

    def step(lo, hi, mid):
        up = count(lambda s, c: s > mid) >= TOP_K
        return jnp.where(up, mid, lo), jnp.where(up, hi, mid)

    def coarse(_, c):
        lo, hi = c
        return step(lo, hi, 0.5 * (lo + hi))

    lo, hi = lax.fori_loop(0, BISECT_ITERS, coarse, (lo, hi))

    def bounds(lo, hi):
        def fn(acc, s, c):
            t_lo, t_hi = acc
            t_lo = jnp.minimum(t_lo, jnp.min(jnp.where(s > lo, s, BIG), axis=0, keepdims=True))
            t_hi = jnp.maximum(t_hi, jnp.max(jnp.where(s <= hi, s, -BIG), axis=0, keepdims=True))
            return t_lo, t_hi
        return scan(fn, (jnp.full((1, nq), BIG, F32), jnp.full((1, nq), -BIG, F32)))

    def unfinished(c):
        _, _, t_lo, t_hi = c
        return _any((t_lo != t_hi) & search)

    def refine(c):
        lo, hi, t_lo, t_hi = c
        mid = 0.5 * (t_lo + t_hi)
        mid = jnp.where(mid >= t_hi, t_lo, mid)
        lo, hi = step(lo, hi, mid)
        return (lo, hi) + bounds(lo, hi)

    _, _, _, thr = lax.while_loop(unfinished, refine, (lo, hi) + bounds(lo, hi))
    return thr, TOP_K - count(lambda s, c: s > thr)


def _dsa_kernel(aq_ref, iq_ref, iw_ref, ak_ref, av_ref, ik_ref, o_ref, k_s, vt_s, ik_s, sc_s, acc_s):
    j = pl.program_id(1)
    nb = aq_ref.shape[0]
    lp = ak_ref.shape[1]
    kt = KEY_TILE
    n_full, tail = divmod(lp, kt)
    pairs = A_HEADS // 2
    rows = range(nb)

    @pl.when(j == 0)
    def _():
        def put(r, c, k, ik, v):
            k_s[r, c] = k
            ik_s[r, c] = ik
            v = v.astype(F32)
            for p in range(pairs):
                vt_s[r, c, p] = v[:, p * LANES:(p + 1) * LANES].T.astype(BF16)

        def copy_tile(c, carry):
            at = pl.ds(pl.multiple_of(c * kt, kt), kt)
            for r in rows:
                put(r, c, ak_ref[r, at, :], ik_ref[r, at, :], av_ref[r, at, :])
            return carry

        lax.fori_loop(0, n_full, copy_tile, 0)
        padded = lambda a: jnp.concatenate([a, jnp.zeros((kt - tail, a.shape[1]), a.dtype)], axis=0)
        at = slice(n_full * kt, lp)
        zeros = lambda ref: jnp.zeros(ref.shape[2:], BF16)
        for r in rows:
            if tail:
                put(r, n_full, padded(ak_ref[r, at, :]), padded(ik_ref[r, at, :]),
                    padded(av_ref[r, at, :]))
            for c in range(n_full + (tail > 0), k_s.shape[1]):
                put(r, c, zeros(k_s), zeros(ik_s), zeros(k_s))
        sc_s[...] = jnp.full(sc_s.shape, NEG, F32)

    low_half = _low_half()
    q_row = j * kt + lax.broadcasted_iota(jnp.int32, (kt, 1), 0)
    q_col = j * kt + lax.broadcasted_iota(jnp.int32, (1, kt), 1)
    in_range = q_row < lp

    def both_heads(x, p):
        x = x[:, p * LANES:(p + 1) * LANES]
        return jnp.concatenate([_keep_head(x, 0, low_half), _keep_head(x, 1, low_half)], axis=0)

    aq_p, iq_p, iw_t = [], [], []
    for r in rows:
        aq = jnp.where(in_range, aq_ref[r], jnp.zeros_like(aq_ref[r]))
        iq = jnp.where(in_range, iq_ref[r], jnp.zeros_like(iq_ref[r]))
        aq_p.append([both_heads(aq, p) for p in range(pairs)])
        iq_p.append([both_heads(iq, p) for p in range(pairs)])
        iw_t.append(jnp.where(in_range, iw_ref[r], 0.0).T)
    key_grid = lax.broadcasted_iota(jnp.int32, (kt, kt), 0)
    key_limit = jnp.minimum((q_col // CHUNK + 1) * CHUNK, lp)
    n_groups = (j + KEY_UNROLL) // KEY_UNROLL
    cols = lambda r: slice(r * kt, (r + 1) * kt)

    def score_tiles(g, carry):
        mn, mx = carry
        tiles = [g * KEY_UNROLL + u for u in range(KEY_UNROLL)]
        logits = [[[lax.dot_general(ik_s[r, c], iq_p[r][p], _NT, preferred_element_type=F32)
                    for p in range(pairs)] for r in rows] for c in tiles]
        mn, mx = list(mn), list(mx)
        for c, per_row in zip(tiles, logits):
            key = key_grid + c * kt
            admissible = (key >= jnp.where(c == 0, PAD, 0)) & (key < key_limit)
            for r, lg in zip(rows, per_row):
                sc = jnp.zeros((kt, kt), F32)
                for p in range(pairs):
                    for half in range(2):
                        h = 2 * p + half
                        sc = sc + iw_t[r][h:h + 1, :] * jnp.maximum(
                            lg[p][:, half * kt:(half + 1) * kt], 0.0)
                sc_s[c, :, cols(r)] = jnp.where(admissible, sc, NEG)
                mn[r] = jnp.minimum(mn[r], jnp.min(jnp.where(admissible, sc, BIG), axis=0, keepdims=True))
                mx[r] = jnp.maximum(mx[r], jnp.max(jnp.where(admissible, sc, NEG), axis=0, keepdims=True))
        return tuple(mn), tuple(mx)

    mn, mx = lax.fori_loop(0, n_groups, score_tiles,
                           (tuple(jnp.full((1, kt), BIG, F32) for _ in rows),
                            tuple(jnp.full((1, kt), NEG, F32) for _ in rows)))
    mn = jnp.concatenate(mn, axis=1)
    mx = jnp.concatenate(mx, axis=1)

    first_search_block = (TOP_K - N_META + CHUNK - 1) // CHUNK * CHUNK // kt

    tri = (lax.broadcasted_iota(jnp.int32, (kt, kt), 0)
           >= lax.broadcasted_iota(jnp.int32, (kt, kt), 1)).astype(BF16)

    def to_bias(thr, n_equal):
        def body(g, seen):
            for u in range(KEY_UNROLL):
                c = g * KEY_UNROLL + u
                s = sc_s[c]
                equal = s == thr
                rank = seen + jnp.dot(tri, jnp.where(equal, 1.0, 0.0).astype(BF16),
                                      preferred_element_type=F32)
                sc_s[c] = jnp.where((s > thr) | (equal & (rank <= n_equal)), 0.0, NEG)
                seen = rank[kt - 1:kt, :]
            return seen
        lax.fori_loop(0, n_groups, body, jnp.zeros((1, nb * kt), F32))

    @pl.when(j < first_search_block)
    def _():
        to_bias(jnp.full((1, nb * kt), 0.5 * NEG, F32), jnp.zeros((1, nb * kt), F32))

    @pl.when(j >= first_search_block)
    def _():
        search = jnp.concatenate([q_col < lp] * nb, axis=1)
        to_bias(*_topk_threshold(sc_s, n_groups, mn - (jnp.abs(mn) + 1.0), mx, search))

    acc_s[...] = jnp.zeros(acc_s.shape, F32)
    chains = [(r, p) for r in rows for p in range(pairs)]

    def attend(g, carry):
        m, l = carry
        c0 = g * KEY_UNROLL
        s = []
        for r in rows:
            bias = jnp.concatenate([sc_s[c0 + u, :, cols(r)] for u in range(KEY_UNROLL)], axis=0)
            bias = jnp.concatenate([bias, bias], axis=1)
            for p in range(pairs):
                k = jnp.concatenate([k_s[r, c0 + u, :, p * LANES:(p + 1) * LANES]
                                     for u in range(KEY_UNROLL)], axis=0)
                s.append(lax.dot_general(k, aq_p[r][p], _NT, preferred_element_type=F32) + bias)
        new_m, new_l = [], []
        for i, (r, p) in enumerate(chains):
            m_i = jnp.maximum(m[i], jnp.max(s[i], axis=0, keepdims=True))
            alpha = jnp.exp2(m[i] - m_i)
            pr = jnp.exp2(s[i] - m_i)
            new_l.append(alpha * l[i] + jnp.sum(pr, axis=0, keepdims=True))
            vt = jnp.concatenate([vt_s[r, c0 + u, p] for u in range(KEY_UNROLL)], axis=1)
            acc_s[r, p] = alpha * acc_s[r, p] + jnp.dot(vt, pr.astype(BF16), preferred_element_type=F32)
            new_m.append(m_i)
        return tuple(new_m), tuple(new_l)

    init = (tuple(jnp.full((1, 2 * kt), -BIG, F32) for _ in chains),
            tuple(jnp.zeros((1, 2 * kt), F32) for _ in chains))
    _, l = lax.fori_loop(0, n_groups, attend, init)

    v_row_low = lax.broadcasted_iota(jnp.int32, (LANES, 1), 0) < HEAD_DIM
    for i, (r, p) in enumerate(chains):
        out_t = acc_s[r, p] / l[i]
        out_t = jnp.where(v_row_low, out_t[:, 0:kt], out_t[:, kt:2 * kt])
        o_ref[r, :, p * LANES:(p + 1) * LANES] = jnp.where(q_row >= PAD, out_t.T, 0.0).astype(BF16)


def _dsa(aq, ak, av, iq, ik, iw):
    b, lp, width = aq.shape
    kt = KEY_TILE
    nb = DSA_BATCH
    n_tiles = -(-lp // kt)
    n_scan = -(-n_tiles // KEY_UNROLL) * KEY_UNROLL
    q_spec = lambda n: pl.BlockSpec((nb, kt, n), lambda bi, j: (bi, j, 0))
    full_spec = lambda n: pl.BlockSpec((nb, lp, n), lambda bi, j: (bi, 0, 0))
    return pl.pallas_call(
        _dsa_kernel,
        grid=(b // nb, n_tiles),
        in_specs=[q_spec(width), q_spec(width), q_spec(LANES), full_spec(width), full_spec(width),
                  full_spec(LANES)],
        out_specs=q_spec(width),
        out_shape=jax.ShapeDtypeStruct((b, lp, width), BF16),
        scratch_shapes=[
            pltpu.VMEM((nb, n_scan, kt, width), BF16),
            pltpu.VMEM((nb, n_scan, A_HEADS // 2, LANES, kt), BF16),
            pltpu.VMEM((nb, n_scan, kt, LANES), BF16),
            pltpu.VMEM((n_scan, kt, nb * kt), F32),
            pltpu.VMEM((nb, A_HEADS // 2, LANES, 2 * kt), F32),
        ],
        compiler_params=_params(("parallel", "arbitrary")),
        name="dsa_attention",
    )(aq, iq, iw, ak, av, ik)


def _retention_kernel(q_ref, k_ref, v_ref, g_ref, dmat_ref, qdec_ref, kdec_ref, cdec_ref, o_ref,
                      state):
    i = pl.program_id(1)

    @pl.when(i == 0)
    def _():
        state[...] = jnp.zeros(state.shape, F32)

    low_half = _low_half()
    for pair in range(B_HEADS // 2):
        sl = slice(pair * LANES, (pair + 1) * LANES)
        for r in range(q_ref.shape[0]):
            q_pair = q_ref[r, :, sl]
            k_pair = k_ref[r, :, sl]
            s_old = state[r, pair]
            s_new = s_old * cdec_ref[pair]
            for half in range(2):
                h = 2 * pair + half
                vsl = slice(h * B_VDIM, (h + 1) * B_VDIM)
                v = v_ref[r, :, vsl]
                q = _keep_head(q_pair, half, low_half)
                inner = lax.dot_general(q, k_pair, _NT, preferred_element_type=F32) * dmat_ref[h]
                y = jnp.dot(inner.astype(BF16), v, preferred_element_type=F32)
                y = y + jnp.dot(q, s_old.astype(BF16), preferred_element_type=F32) * qdec_ref[h]
                kd = _keep_head((k_pair.astype(F32) * kdec_ref[h]).astype(BF16), half, low_half)
                s_new = s_new + lax.dot_general(kd, v, _TN, preferred_element_type=F32)
                o_ref[r, :, vsl] = (_rms(y) * jax.nn.silu(g_ref[r, :, vsl])).astype(BF16)
            state[r, pair] = s_new


def _retention(q, k, v, gate):
    b, lp, _ = q.shape
    t = RET_TILE
    log_g = jnp.log(1.0 - 2.0 ** (-5.0 - jnp.arange(B_HEADS, dtype=F32)))
    idx = jnp.arange(t, dtype=F32)
    rel = idx[:, None] - idx[None, :]
    dmat = jnp.where(rel >= 0, jnp.exp(jnp.maximum(rel, 0.0)[None] * log_g[:, None, None]), 0.0)
    qdec = jnp.exp((idx + 1.0)[None, :, None] * log_g[:, None, None])
    kdec = jnp.exp((t - 1.0 - idx)[None, :, None] * log_g[:, None, None])
    cdec = jnp.repeat(jnp.exp(t * log_g), HEAD_DIM).reshape(B_HEADS // 2, LANES, 1)

    qk_w, v_w = B_HEADS * HEAD_DIM, B_HEADS * B_VDIM
    nb = MIX_BATCH
    seq = lambda n: pl.BlockSpec((nb, t, n), lambda bi, i: (bi, i, 0))
    return pl.pallas_call(
        _retention_kernel,
        grid=(b // nb, lp // t),
        in_specs=[seq(qk_w), seq(qk_w), seq(v_w), seq(v_w),
                  _resident((B_HEADS, t, t)), _resident((B_HEADS, t, 1)),
                  _resident((B_HEADS, t, 1)), _resident((B_HEADS // 2, LANES, 1))],
        out_specs=seq(v_w),
        out_shape=jax.ShapeDtypeStruct((b, lp, v_w), BF16),
        scratch_shapes=[pltpu.VMEM((nb, B_HEADS // 2, LANES, B_VDIM), F32)],
        compiler_params=_params(("parallel", "arbitrary")),
        name="retention",
    )(q, k, v, gate, dmat, qdec, kdec, cdec)


def _swa_kernel(q_ref, k_ref, v_ref, sink_ref, o_ref):
    j = pl.program_id(1)
    nb, tq = q_ref.shape[0], q_ref.shape[1]
    lp = k_ref.shape[1]
    win = tq + C_WIN_CHUNKS * CHUNK
    rep = C_HEADS // C_KV_HEADS
    low_half = _low_half()

    start = pl.multiple_of(jnp.clip(j * tq - C_WIN_CHUNKS * CHUNK, 0, lp - win), CHUNK)
    q_row = j * tq + lax.broadcasted_iota(jnp.int32, (tq, 1), 0)
    q_col = j * tq + lax.broadcasted_iota(jnp.int32, (1, tq), 1)
    q_chunk = jnp.concatenate([q_col // CHUNK] * rep, axis=1)
    key = lax.broadcasted_iota(jnp.int32, (CHUNK + win, 1), 0)
    k_chunk = start // CHUNK + (key - CHUNK) // CHUNK
    in_window = ((key >= CHUNK) & (k_chunk >= 1) & (k_chunk <= q_chunk)
                 & (k_chunk >= q_chunk - C_WIN_CHUNKS))
    valid = in_window | ((key < CHUNK) & (key >= PAD))
    v_row_low = lax.broadcasted_iota(jnp.int32, (LANES, 1), 0) < HEAD_DIM

    for r in range(nb):
        q = jnp.where(q_row < lp, q_ref[r], jnp.zeros_like(q_ref[r]))
        for g in range(C_KV_HEADS):
            sl = slice(g * LANES, (g + 1) * LANES)
            keys = jnp.concatenate([k_ref[r, 0:CHUNK, sl], k_ref[r, pl.ds(start, win), sl]], axis=0)
            vals = jnp.concatenate([v_ref[r, 0:CHUNK, sl], v_ref[r, pl.ds(start, win), sl]], axis=0)
            heads = range(g * rep, (g + 1) * rep)
            qs = jnp.concatenate(
                [_keep_head(q[:, (h // 2) * LANES:(h // 2 + 1) * LANES], h % 2, low_half)
                 for h in heads], axis=0)
            sink = jnp.concatenate(
                [jnp.broadcast_to(sink_ref[0:1, h:h + 1] * LOG2E, (1, tq)) for h in heads], axis=1)
            s = jnp.where(valid, lax.dot_general(keys, qs, _NT, preferred_element_type=F32), NEG)
            m = jnp.maximum(jnp.max(s, axis=0, keepdims=True), sink)
            p = jnp.exp2(s - m)
            denom = jnp.sum(p, axis=0, keepdims=True) + jnp.exp2(sink - m)
            y_t = lax.dot_general(vals, p.astype(BF16), _TN, preferred_element_type=F32) / denom
            for u in range(rep // 2):
                out_t = jnp.where(v_row_low, y_t[:, 2 * u * tq:(2 * u + 1) * tq],
                                  y_t[:, (2 * u + 1) * tq:(2 * u + 2) * tq])
                pair = g * rep // 2 + u
                o_ref[r, :, pair * LANES:(pair + 1) * LANES] = jnp.where(
                    q_row >= PAD, out_t.T, 0.0).astype(BF16)


def _swa(q, k, v, sinks):
    b, lp, width = q.shape
    tq = KEY_TILE
    nb = MIX_BATCH
    kv = pl.BlockSpec((nb, lp, k.shape[2]), lambda bi, j: (bi, 0, 0))
    return pl.pallas_call(
        _swa_kernel,
        grid=(b // nb, -(-lp // tq)),
        in_specs=[pl.BlockSpec((nb, tq, width), lambda bi, j: (bi, j, 0)), kv, kv,
                  _resident((1, C_HEADS))],
        out_specs=pl.BlockSpec((nb, tq, width), lambda bi, j: (bi, j, 0)),
        out_shape=jax.ShapeDtypeStruct((b, lp, width), BF16),
        compiler_params=_params(("parallel", "parallel")),
        name="swa_sinks",
    )(q, k, v, sinks.reshape(1, C_HEADS))


def _pool_kernel(x_ref, mix_ref, scale_ref, o_ref):
    lp = x_ref.shape[1]
    row = lax.broadcasted_iota(jnp.int32, (lp, 1), 0)
    t = row - PAD
    outs = []
    for gi, w in enumerate(POOL_WINDOWS):
        x = x_ref[0, :, gi * D_GROUP:(gi + 1) * D_GROUP]
        s, shift = x, 1
        while shift < w:
            s = s + jnp.where(row >= shift, pltpu.roll(s, shift, 0), 0.0)
            shift *= 2
        count = jnp.maximum(jnp.minimum(t + 1, w), 1).astype(F32)
        y = (s / count - x).astype(BF16)
        outs.append(jnp.dot(y, mix_ref[gi].astype(BF16), preferred_element_type=F32))
    out = jnp.concatenate(outs, axis=-1) * scale_ref[...]
    o_ref[0] = jnp.where(row >= PAD, out, 0.0).astype(BF16)


def _pool(x, mix, scale):
    b, lp, width = x.shape
    n_groups = len(POOL_WINDOWS)
    return pl.pallas_call(
        _pool_kernel,
        grid=(b,),
        in_specs=[pl.BlockSpec((1, lp, width), lambda bi: (bi, 0, 0)),
                  _resident((n_groups, D_GROUP, D_GROUP)), _resident((1, width))],
        out_specs=pl.BlockSpec((1, lp, width), lambda bi: (bi, 0, 0)),
        out_shape=jax.ShapeDtypeStruct((b, lp, width), BF16),
        compiler_params=_params(("parallel",)),
        name="pool_mixer",
    )(x, mix, scale.reshape(1, width))


def kernel(x, meta_tokens, ffn1_norm, ffn1_w_in, ffn1_w_out, mix_norm, ffn2_norm, ffn2_w_in,
           ffn2_w_out, ev_w_in, ev_a_q_norm, ev_a_k_norm, ev_w_out, od_w_in, od_c_q_norm,
           od_c_k_norm, od_c_sinks, od_d_mix, od_d_scale, od_w_out):
    b, s, d = x.shape
    lp = PAD + N_META + s
    assert d == D_MODEL and (b * lp) % ROW_TILE == 0 and s % ROW_TILE == 0
    assert lp % PROJ_TILE == 0 and lp % RET_TILE == 0
    assert min(TOP_K, s // 4) == TOP_K and b % DSA_BATCH == 0 and b % MIX_BATCH == 0
    depth = ffn1_norm.shape[0]
    seq = lambda a: a.reshape(b, lp, a.shape[-1])
    flat = lambda a: a.reshape(b * lp, a.shape[-1])

    ffn1_w_in, ffn1_w_out, ffn2_w_in, ffn2_w_out = (
        w.astype(BF16) for w in (ffn1_w_in, ffn1_w_out, ffn2_w_in, ffn2_w_out))
    for layer in range(depth):
        if layer == 0:
            h = _ffn_first(x, meta_tokens, ffn1_norm[0], ffn1_w_in, ffn1_w_out, 0)
        else:
            h = _ffn(h, ffn1_norm[layer], ffn1_w_in, ffn1_w_out, layer)
        if layer % 2 == 0:
            e = layer // 2
            aq, ak, av, iq, ik, iw, bq, bk, bv, bg = _even_proj(
                seq(h), mix_norm[layer], ev_w_in[e], ev_a_q_norm[e], ev_a_k_norm[e])
            ya = _dsa(aq, ak, av, iq, ik, iw)
            yb = _retention(bq, bk, bv, bg)
            mix = (flat(ya), flat(yb), ev_w_out[e].astype(BF16))
        else:
            o = layer // 2
            cq, ck, cv, dx = _odd_proj(seq(h), mix_norm[layer], od_w_in[o], od_c_q_norm[o],
                                       od_c_k_norm[o])
            yc = _swa(cq, ck, cv, od_c_sinks[o])
            yd = _pool(dx, od_d_mix[o], od_d_scale[o])
            mix = (flat(yc), flat(yd), od_w_out[o].astype(BF16))
        last = layer == depth - 1
        h = _ffn(h, ffn2_norm[layer], ffn2_w_in, ffn2_w_out, layer, (b, s) if last else None, mix)
    return h
```

```python
import functools

import jax
import jax.numpy as jnp
from jax import lax
from jax.experimental import pallas as pl
from jax.experimental.pallas import tpu as pltpu

F32 = jnp.float32
BF16 = jnp.bfloat16

D_MODEL = 1024
D_FF = 2816
CHUNK = 64
N_META = 16
PAD = CHUNK - N_META
HEAD_DIM = 64
LANES = 128
A_HEADS = 8
IDX_HEADS = 8
IDX_DIM = 64
TOP_K = 256
B_HEADS = 8
B_VDIM = 128
C_HEADS = 8
C_KV_HEADS = 2
C_WIN_CHUNKS = 2
D_WIDTH = 512
POOL_WINDOWS = (2, 4, 8, 16)
D_GROUP = 128
EPS = 1e-6
NEG = -1e30
BIG = 3e38
LOG2E = 1.4426950408889634

VMEM_LIMIT = 56 * 1024 * 1024
ROW_TILE = 512
FF_CHUNK = 256
PROJ_TILE = 704
KEY_TILE = 128
KEY_UNROLL = 4
DSA_BATCH = 2
MIX_BATCH = 2
RET_TILE = 528
BISECT_ITERS = 15

_NT = (((1,), (1,)), ((), ()))
_TN = (((0,), (0,)), ((), ()))


def _params(sem):
    return pltpu.CompilerParams(dimension_semantics=sem, vmem_limit_bytes=VMEM_LIMIT)


def _rms(x, gain=None):
    y = x * lax.rsqrt(jnp.mean(x * x, axis=-1, keepdims=True) + EPS)
    return y if gain is None else y * gain


def _resident(shape):
    return pl.BlockSpec(shape, lambda *_: (0,) * len(shape), pipeline_mode=pl.Buffered(1))


def _low_half():
    return lax.broadcasted_iota(jnp.int32, (1, LANES), 1) < HEAD_DIM


def _keep_head(x, half, low_half):
    return jnp.where(low_half if half == 0 else jnp.logical_not(low_half), x, jnp.zeros_like(x))


def _ffn_kernel(h_ref, g_ref, win_ref, wout_ref, o_ref, mix_refs=None):
    x = h_ref[...]
    if mix_refs is not None:
        ya_ref, yb_ref, wmix_ref = mix_refs
        na = ya_ref.shape[1]
        x = x + jnp.dot(ya_ref[...], wmix_ref[0:na, :], preferred_element_type=F32)
        x = x + jnp.dot(yb_ref[...], wmix_ref[na:, :], preferred_element_type=F32)
    xn = _rms(x, g_ref[...]).astype(BF16)
    acc = jnp.zeros(x.shape, F32)
    for c in range(D_FF // FF_CHUNK):
        lo = c * FF_CHUNK
        g = jnp.dot(xn, win_ref[:, lo:lo + FF_CHUNK], preferred_element_type=F32)
        u = jnp.dot(xn, win_ref[:, D_FF + lo:D_FF + lo + FF_CHUNK], preferred_element_type=F32)
        a = (jax.nn.silu(g) * u).astype(BF16)
        acc = acc + jnp.dot(a, wout_ref[lo:lo + FF_CHUNK, :], preferred_element_type=F32)
    o_ref[...] = x + 0.5 * acc


def _mix_ffn_kernel(h_ref, ya_ref, yb_ref, wmix_ref, g_ref, win_ref, wout_ref, o_ref):
    _ffn_kernel(h_ref, g_ref, win_ref, wout_ref, o_ref, (ya_ref, yb_ref, wmix_ref))


def _layer_weights(shape, layer):
    return pl.BlockSpec((None,) + shape, lambda *_: (layer, 0, 0), pipeline_mode=pl.Buffered(1))


def _ffn_weight_specs(layer):
    return [_resident((1, D_MODEL)), _layer_weights((D_MODEL, 2 * D_FF), layer),
            _layer_weights((D_FF, D_MODEL), layer)]


def _frame_rows(b, lp, s, n):
    return pl.BlockSpec(
        (pl.Element(ROW_TILE), pl.Element(n)),
        lambda bi, t: (pl.multiple_of(bi * lp + lp - s + ROW_TILE * t, CHUNK), 0))


def _ffn(h, gain, w_in, w_out, layer, frames_out=None, mix=None):
    if frames_out is None:
        rows = h.shape[0]
        grid = (rows // ROW_TILE,)
        row_spec = lambda n: pl.BlockSpec((ROW_TILE, n), lambda i: (i, 0))
        out_spec = row_spec(D_MODEL)
        out_shape = jax.ShapeDtypeStruct((rows, D_MODEL), F32)
        sem = ("parallel",)
    else:
        b, s = frames_out
        grid = (b, s // ROW_TILE)
        row_spec = functools.partial(_frame_rows, b, h.shape[0] // b, s)
        out_spec = pl.BlockSpec((None, ROW_TILE, D_MODEL), lambda bi, t: (bi, t, 0))
        out_shape = jax.ShapeDtypeStruct((b, s, D_MODEL), F32)
        sem = ("parallel", "parallel")
    if mix is None:
        body, rows_in, specs = _ffn_kernel, [h], [row_spec(D_MODEL)]
    else:
        ya, yb, w_mix = mix
        body, rows_in = _mix_ffn_kernel, [h, ya, yb, w_mix]
        specs = [row_spec(D_MODEL), row_spec(ya.shape[1]), row_spec(yb.shape[1]),
                 _resident(w_mix.shape)]
    return pl.pallas_call(
        body,
        grid=grid,
        in_specs=specs + _ffn_weight_specs(layer),
        out_specs=out_spec,
        out_shape=out_shape,
        compiler_params=_params(sem),
        name="ffn",
    )(*rows_in, gain.reshape(1, D_MODEL), w_in, w_out)


def _ffn_meta_kernel(c_ref, g_ref, win_ref, wout_ref, h_hbm_ref, o_ref, y_s):
    @pl.when(pl.program_id(0) == 0)
    def _():
        _ffn_kernel(c_ref, g_ref, win_ref, wout_ref, y_s)

    o_ref[...] = y_s[...]


def _ffn_first(x, meta_tokens, gain, w_in, w_out, layer):
    b, s, _ = x.shape
    lp = PAD + N_META + s
    tiles = s // ROW_TILE
    gain = gain.reshape(1, D_MODEL)
    h = pl.pallas_call(
        _ffn_kernel,
        grid=(b, tiles),
        in_specs=[pl.BlockSpec((ROW_TILE, D_MODEL), lambda bi, t: (bi * tiles + t, 0))]
        + _ffn_weight_specs(layer),
        out_specs=_frame_rows(b, lp, s, D_MODEL),
        out_shape=jax.ShapeDtypeStruct((b * lp, D_MODEL), F32),
        compiler_params=_params(("parallel", "parallel")),
        name="ffn_frames",
    )(x.reshape(b * s, D_MODEL), gain, w_in, w_out)
    chunk0 = jnp.concatenate([jnp.zeros((PAD, D_MODEL), x.dtype), meta_tokens.astype(x.dtype)], axis=0)
    return pl.pallas_call(
        _ffn_meta_kernel,
        grid=(b,),
        in_specs=[_resident((CHUNK, D_MODEL))] + _ffn_weight_specs(layer)
        + [pl.BlockSpec(memory_space=pl.ANY)],
        out_specs=pl.BlockSpec((CHUNK, D_MODEL), lambda bi: (bi * (lp // CHUNK), 0)),
        out_shape=jax.ShapeDtypeStruct((b * lp, D_MODEL), F32),
        scratch_shapes=[pltpu.VMEM((CHUNK, D_MODEL), F32)],
        input_output_aliases={4: 0},
        compiler_params=_params(("arbitrary",)),
        name="ffn_meta",
    )(chunk0, gain, w_in, w_out, h)


def _head_norm(y, seg_ref, gain):
    sq = (y * y).astype(BF16)
    ms = jnp.concatenate(
        [jnp.dot(sq[:, p * LANES:(p + 1) * LANES], seg_ref[...], preferred_element_type=F32)
         for p in range(y.shape[1] // LANES)], axis=-1)
    return y * lax.rsqrt(ms + EPS) * gain


def _rotate_half(x, first_half):
    n = x.shape[-1]
    half = HEAD_DIM // 2
    return jnp.where(first_half, -pltpu.roll(x, n - half, 1), pltpu.roll(x, half, 1))


def _even_proj_kernel(h_ref, g_ref, w_ref, seg_ref, qg_ref, kg_ref, cos_ref, sin_ref,
                      aq_ref, ak_ref, av_ref, iq_ref, ik_ref, iw_ref, bq_ref, bk_ref, bv_ref,
                      bg_ref):
    xn = _rms(h_ref[0], g_ref[...]).astype(BF16)
    off = [0]

    def cols(n):
        y = jnp.dot(xn, w_ref[:, off[0]:off[0] + n], preferred_element_type=F32)
        off[0] += n
        return y

    w = A_HEADS * HEAD_DIM
    aq_ref[0] = _head_norm(cols(w), seg_ref, qg_ref[...]).astype(BF16)
    ak_ref[0] = _head_norm(cols(w), seg_ref, kg_ref[...]).astype(BF16)
    av_ref[0] = cols(w).astype(BF16)
    iq_ref[0] = cols(IDX_HEADS * IDX_DIM).astype(BF16)
    ik_ref[0] = cols(LANES).astype(BF16)
    iw_ref[0] = cols(LANES) * (IDX_HEADS ** -0.5 * IDX_DIM ** -0.5)
    cos, sin = cos_ref[...], sin_ref[...]
    first_half = lax.broadcasted_iota(jnp.int32, (1, w), 1) % HEAD_DIM < HEAD_DIM // 2
    q = cols(w)
    bq_ref[0] = (q * cos + _rotate_half(q, first_half) * sin).astype(BF16)
    k = cols(w)
    bk_ref[0] = ((k * cos + _rotate_half(k, first_half) * sin) * HEAD_DIM ** -0.5).astype(BF16)
    for c in range(2):
        sl = slice(c * w, (c + 1) * w)
        bv_ref[0, :, sl] = cols(w).astype(BF16)
    for c in range(2):
        sl = slice(c * w, (c + 1) * w)
        bg_ref[0, :, sl] = cols(w)


def _odd_proj_kernel(h_ref, g_ref, w_ref, seg_ref, qg_ref, kg_ref, cq_ref, ck_ref, cv_ref, dx_ref):
    xn = _rms(h_ref[0], g_ref[...]).astype(BF16)
    off = [0]

    def cols(n):
        y = jnp.dot(xn, w_ref[:, off[0]:off[0] + n], preferred_element_type=F32)
        off[0] += n
        return y

    kv_w = 2 * C_KV_HEADS * HEAD_DIM
    cq_ref[0] = _head_norm(cols(C_HEADS * HEAD_DIM), seg_ref, qg_ref[...]).astype(BF16)
    ck_ref[0] = _head_norm(cols(kv_w), seg_ref, kg_ref[...]).astype(BF16)
    cv_ref[0] = cols(kv_w).astype(BF16)
    dx_ref[0] = cols(D_WIDTH)


def _seg_matrix():
    g = jnp.arange(LANES) // HEAD_DIM
    return jnp.where(g[:, None] == g[None, :], 1.0 / HEAD_DIM, 0.0).astype(BF16)


def _proj_call(body, name, h, gain, w, extra_in, extra_specs, outs):
    b, lp, _ = h.shape
    t = PROJ_TILE
    seq = lambda n: pl.BlockSpec((1, t, n), lambda bi, i: (bi, i, 0))
    return pl.pallas_call(
        body,
        grid=(b, lp // t),
        in_specs=[seq(D_MODEL), _resident((1, D_MODEL)), _resident(w.shape),
                  _resident((LANES, LANES))] + extra_specs,
        out_specs=[seq(n) for n, _ in outs],
        out_shape=[jax.ShapeDtypeStruct((b, lp, n), dt) for n, dt in outs],
        compiler_params=_params(("parallel", "parallel")),
        name=name,
    )(h, gain.reshape(1, D_MODEL), w, _seg_matrix(), *extra_in)


def _rotary_tables(lp):
    pos = (jnp.arange(lp) - PAD).astype(F32)
    inv = 1.0 / (10000.0 ** (jnp.arange(0, HEAD_DIM, 2, dtype=F32) / HEAD_DIM))
    ang = pos[:, None] * inv[None]
    return (jnp.tile(jnp.cos(ang), (1, 2 * B_HEADS)), jnp.tile(jnp.sin(ang), (1, 2 * B_HEADS)))


def _even_proj(h, gain, w, q_gain, k_gain):
    lp = h.shape[1]
    w = w.astype(BF16)
    a_w = 3 * A_HEADS * HEAD_DIM + IDX_HEADS * IDX_DIM
    ik = w[:, a_w:a_w + IDX_DIM]
    iw = jnp.pad(w[:, a_w + IDX_DIM:a_w + IDX_DIM + IDX_HEADS], ((0, 0), (0, LANES - IDX_HEADS)))
    w = jnp.concatenate([w[:, :a_w], ik, ik, iw, w[:, a_w + IDX_DIM + IDX_HEADS:]], axis=1)
    cos, sin = _rotary_tables(lp)
    w_heads = A_HEADS * HEAD_DIM
    qg = jnp.tile(q_gain, A_HEADS).reshape(1, w_heads) * (HEAD_DIM ** -0.5 * LOG2E)
    kg = jnp.tile(k_gain, A_HEADS).reshape(1, w_heads)
    table = pl.BlockSpec((PROJ_TILE, w_heads), lambda bi, i: (i, 0))
    outs = [(w_heads, BF16)] * 4 + [(LANES, BF16), (LANES, F32), (w_heads, BF16), (w_heads, BF16),
                                    (B_HEADS * B_VDIM, BF16), (B_HEADS * B_VDIM, F32)]
    return _proj_call(_even_proj_kernel, "even_in_proj", h, gain, w,
                      [qg, kg, cos, sin],
                      [_resident((1, w_heads)), _resident((1, w_heads)), table, table], outs)


def _odd_proj(h, gain, w, q_gain, k_gain):
    w = w.astype(BF16)
    q_w = C_HEADS * HEAD_DIM
    kv_w = C_KV_HEADS * HEAD_DIM
    dup = lambda m: jnp.concatenate(
        [m[:, g * HEAD_DIM:(g + 1) * HEAD_DIM] for g in range(C_KV_HEADS) for _ in range(2)], axis=1)
    w = jnp.concatenate([w[:, :q_w], dup(w[:, q_w:q_w + kv_w]), dup(w[:, q_w + kv_w:q_w + 2 * kv_w]),
                         w[:, q_w + 2 * kv_w:]], axis=1)
    qg = jnp.tile(q_gain, C_HEADS).reshape(1, q_w) * (HEAD_DIM ** -0.5 * LOG2E)
    kg = jnp.tile(k_gain, 2 * C_KV_HEADS).reshape(1, 2 * kv_w)
    outs = [(q_w, BF16), (2 * kv_w, BF16), (2 * kv_w, BF16), (D_WIDTH, F32)]
    return _proj_call(_odd_proj_kernel, "odd_in_proj", h, gain, w, [qg, kg],
                      [_resident((1, q_w)), _resident((1, 2 * kv_w))], outs)


def _fold(x):
    return x.reshape(x.shape[0] // 8, 8, x.shape[1]).sum(axis=0)


def _any(mask):
    return jnp.max(jnp.where(mask, 1.0, 0.0)) > 0.0


def _topk_threshold(sc_s, n_groups, lo, hi, search):
    nq = sc_s.shape[2]

    def scan(fn, init):
        def body(g, acc):
            for u in range(KEY_UNROLL):
                c = g * KEY_UNROLL + u
                acc = fn(acc, sc_s[c], c)
            return acc
        return lax.fori_loop(0, n_groups, body, init)

    def count(pred):
        acc = scan(lambda a, s, c: a + _fold(jnp.where(pred(s, c), 1.0, 0.0)), jnp.zeros((8, nq), F32))
        return jnp.sum(acc, axis=0, keepdims=True)

    def step(lo, hi, mid):
        up = count(lambda s, c: s > mid) >= TOP_K
        return jnp.where(up, mid, lo), jnp.where(up, hi, mid)

    def coarse(_, c):
        lo, hi = c
        return step(lo, hi, 0.5 * (lo + hi))

    lo, hi = lax.fori_loop(0, BISECT_ITERS, coarse, (lo, hi))

    def bounds(lo, hi):
        def fn(acc, s, c):
            t_lo, t_hi = acc
            t_lo = jnp.minimum(t_lo, jnp.min(jnp.where(s > lo, s, BIG), axis=0, keepdims=True))
            t_hi = jnp.maximum(t_hi, jnp.max(jnp.where(s <= hi, s, -BIG), axis=0, keepdims=True))
            return t_lo, t_hi
        return scan(fn, (jnp.full((1, nq), BIG, F32), jnp.full((1, nq), -BIG, F32)))

    def unfinished(c):
        _, _, t_lo, t_hi = c
        return _any((t_lo != t_hi) & search)

    def refine(c):
        lo, hi, t_lo, t_hi = c
        mid = 0.5 * (t_lo + t_hi)
        mid = jnp.where(mid >= t_hi, t_lo, mid)
        lo, hi = step(lo, hi, mid)
        return (lo, hi) + bounds(lo, hi)

    _, _, _, thr = lax.while_loop(unfinished, refine, (lo, hi) + bounds(lo, hi))
    return thr, TOP_K - count(lambda s, c: s > thr)


def _dsa_kernel(aq_ref, iq_ref, iw_ref, ak_ref, av_ref, ik_ref, o_ref, k_s, vt_s, ik_s, sc_s, acc_s):
    j = pl.program_id(1)
    nb = aq_ref.shape[0]
    lp = ak_ref.shape[1]
    kt = KEY_TILE
    n_full, tail = divmod(lp, kt)
    pairs = A_HEADS // 2
    rows = range(nb)

    @pl.when(j == 0)
    def _():
        def put(r, c, k, ik, v):
            k_s[r, c] = k
            ik_s[r, c] = ik
            v = v.astype(F32)
            for p in range(pairs):
                vt_s[r, c, p] = v[:, p * LANES:(p + 1) * LANES].T.astype(BF16)

        def copy_tile(c, carry):
            at = pl.ds(pl.multiple_of(c * kt, kt), kt)
            for r in rows:
                put(r, c, ak_ref[r, at, :], ik_ref[r, at, :], av_ref[r, at, :])
            return carry

        lax.fori_loop(0, n_full, copy_tile, 0)
        padded = lambda a: jnp.concatenate([a, jnp.zeros((kt - tail, a.shape[1]), a.dtype)], axis=0)
        at = slice(n_full * kt, lp)
        zeros = lambda ref: jnp.zeros(ref.shape[2:], BF16)
        for r in rows:
            if tail:
                put(r, n_full, padded(ak_ref[r, at, :]), padded(ik_ref[r, at, :]),
                    padded(av_ref[r, at, :]))
            for c in range(n_full + (tail > 0), k_s.shape[1]):
                put(r, c, zeros(k_s), zeros(ik_s), zeros(k_s))
        sc_s[...] = jnp.full(sc_s.shape, NEG, F32)

    low_half = _low_half()
    q_row = j * kt + lax.broadcasted_iota(jnp.int32, (kt, 1), 0)
    q_col = j * kt + lax.broadcasted_iota(jnp.int32, (1, kt), 1)
    in_range = q_row < lp

    def both_heads(x, p):
        x = x[:, p * LANES:(p + 1) * LANES]
        return jnp.concatenate([_keep_head(x, 0, low_half), _keep_head(x, 1, low_half)], axis=0)

    aq_p, iq_p, iw_t = [], [], []
    for r in rows:
        aq = jnp.where(in_range, aq_ref[r], jnp.zeros_like(aq_ref[r]))
        iq = jnp.where(in_range, iq_ref[r], jnp.zeros_like(iq_ref[r]))
        aq_p.append([both_heads(aq, p) for p in range(pairs)])
        iq_p.append([both_heads(iq, p) for p in range(pairs)])
        iw_t.append(jnp.where(in_range, iw_ref[r], 0.0).T)
    key_grid = lax.broadcasted_iota(jnp.int32, (kt, kt), 0)
    key_limit = jnp.minimum((q_col // CHUNK + 1) * CHUNK, lp)
    n_groups = (j + KEY_UNROLL) // KEY_UNROLL
    cols = lambda r: slice(r * kt, (r + 1) * kt)

    def score_tiles(g, carry):
        mn, mx = carry
        tiles = [g * KEY_UNROLL + u for u in range(KEY_UNROLL)]
        logits = [[[lax.dot_general(ik_s[r, c], iq_p[r][p], _NT, preferred_element_type=F32)
                    for p in range(pairs)] for r in rows] for c in tiles]
        mn, mx = list(mn), list(mx)
        for c, per_row in zip(tiles, logits):
            key = key_grid + c * kt
            admissible = (key >= jnp.where(c == 0, PAD, 0)) & (key < key_limit)
            for r, lg in zip(rows, per_row):
                sc = jnp.zeros((kt, kt), F32)
                for p in range(pairs):
                    for half in range(2):
                        h = 2 * p + half
                        sc = sc + iw_t[r][h:h + 1, :] * jnp.maximum(
                            lg[p][:, half * kt:(half + 1) * kt], 0.0)
                sc_s[c, :, cols(r)] = jnp.where(admissible, sc, NEG)
                mn[r] = jnp.minimum(mn[r], jnp.min(jnp.where(admissible, sc, BIG), axis=0, keepdims=True))
                mx[r] = jnp.maximum(mx[r], jnp.max(jnp.where(admissible, sc, NEG), axis=0, keepdims=True))
        return tuple(mn), tuple(mx)

    mn, mx = lax.fori_loop(0, n_groups, score_tiles,
                           (tuple(jnp.full((1, kt), BIG, F32) for _ in rows),
                            tuple(jnp.full((1, kt), NEG, F32) for _ in rows)))
    mn = jnp.concatenate(mn, axis=1)
    mx = jnp.concatenate(mx, axis=1)

    first_search_block = (TOP_K - N_META + CHUNK - 1) // CHUNK * CHUNK // kt

    tri = (lax.broadcasted_iota(jnp.int32, (kt, kt), 0)
           >= lax.broadcasted_iota(jnp.int32, (kt, kt), 1)).astype(BF16)

    def to_bias(thr, n_equal):
        def body(g, seen):
            for u in range(KEY_UNROLL):
                c = g * KEY_UNROLL + u
                s = sc_s[c]
                equal = s == thr
                rank = seen + jnp.dot(tri, jnp.where(equal, 1.0, 0.0).astype(BF16),
                                      preferred_element_type=F32)
                sc_s[c] = jnp.where((s > thr) | (equal & (rank <= n_equal)), 0.0, NEG)
                seen = rank[kt - 1:kt, :]
            return seen
        lax.fori_loop(0, n_groups, body, jnp.zeros((1, nb * kt), F32))

    @pl.when(j < first_search_block)
    def _():
        to_bias(jnp.full((1, nb * kt), 0.5 * NEG, F32), jnp.zeros((1, nb * kt), F32))

    @pl.when(j >= first_search_block)
    def _():
        search = jnp.concatenate([q_col < lp] * nb, axis=1)
        to_bias(*_topk_threshold(sc_s, n_groups, mn - (jnp.abs(mn) * 2.0 ** -10 + 1e-30), mx, search))

    acc_s[...] = jnp.zeros(acc_s.shape, F32)
    chains = [(r, p) for r in rows for p in range(pairs)]
    ones = jnp.ones((8, KEY_UNROLL * kt), BF16)

    def attend(g, carry):
        m, l = carry
        c0 = g * KEY_UNROLL
        s = []
        for r in rows:
            bias = jnp.concatenate([sc_s[c0 + u, :, cols(r)] for u in range(KEY_UNROLL)], axis=0)
            bias = jnp.concatenate([bias, bias], axis=1)
            for p in range(pairs):
                k = jnp.concatenate([k_s[r, c0 + u, :, p * LANES:(p + 1) * LANES]
                                     for u in range(KEY_UNROLL)], axis=0)
                s.append(lax.dot_general(k, aq_p[r][p], _NT, preferred_element_type=F32) + bias)
        new_m, new_l = [], []
        for i, (r, p) in enumerate(chains):
            m_i = jnp.maximum(m[i], jnp.max(s[i], axis=0, keepdims=True))
            alpha = jnp.exp2(m[i] - m_i)
            pr = jnp.exp2((s[i] - m_i).astype(BF16))
            new_l.append(alpha * l[i] + jnp.dot(ones, pr, preferred_element_type=F32)[0:1])
            vt = jnp.concatenate([vt_s[r, c0 + u, p] for u in range(KEY_UNROLL)], axis=1)
            acc_s[r, p] = alpha * acc_s[r, p] + jnp.dot(vt, pr, preferred_element_type=F32)
            new_m.append(m_i)
        return tuple(new_m), tuple(new_l)

    init = (tuple(jnp.full((1, 2 * kt), -BIG, F32) for _ in chains),
            tuple(jnp.zeros((1, 2 * kt), F32) for _ in chains))
    _, l = lax.fori_loop(0, n_groups, attend, init)

    v_row_low = lax.broadcasted_iota(jnp.int32, (LANES, 1), 0) < HEAD_DIM
    for i, (r, p) in enumerate(chains):
        out_t = acc_s[r, p] / l[i]
        out_t = jnp.where(v_row_low, out_t[:, 0:kt], out_t[:, kt:2 * kt])
        o_ref[r, :, p * LANES:(p + 1) * LANES] = jnp.where(q_row >= PAD, out_t.T, 0.0).astype(BF16)


def _dsa(aq, ak, av, iq, ik, iw):
    b, lp, width = aq.shape
    kt = KEY_TILE
    nb = DSA_BATCH
    n_tiles = -(-lp // kt)
    n_scan = -(-n_tiles // KEY_UNROLL) * KEY_UNROLL
    q_spec = lambda n: pl.BlockSpec((nb, kt, n), lambda bi, j: (bi, j, 0))
    full_spec = lambda n: pl.BlockSpec((nb, lp, n), lambda bi, j: (bi, 0, 0))
    return pl.pallas_call(
        _dsa_kernel,
        grid=(b // nb, n_tiles),
        in_specs=[q_spec(width), q_spec(width), q_spec(LANES), full_spec(width), full_spec(width),
                  full_spec(LANES)],
        out_specs=q_spec(width),
        out_shape=jax.ShapeDtypeStruct((b, lp, width), BF16),
        scratch_shapes=[
            pltpu.VMEM((nb, n_scan, kt, width), BF16),
            pltpu.VMEM((nb, n_scan, A_HEADS // 2, LANES, kt), BF16),
            pltpu.VMEM((nb, n_scan, kt, LANES), BF16),
            pltpu.VMEM((n_scan, kt, nb * kt), F32),
            pltpu.VMEM((nb, A_HEADS // 2, LANES, 2 * kt), F32),
        ],
        compiler_params=_params(("parallel", "arbitrary")),
        name="dsa_attention",
    )(aq, iq, iw, ak, av, ik)


def _retention_kernel(q_ref, k_ref, v_ref, g_ref, dmat_ref, qdec_ref, kdec_ref, cdec_ref, o_ref,
                      state):
    i = pl.program_id(1)

    @pl.when(i == 0)
    def _():
        state[...] = jnp.zeros(state.shape, F32)

    low_half = _low_half()
    for pair in range(B_HEADS // 2):
        sl = slice(pair * LANES, (pair + 1) * LANES)
        for r in range(q_ref.shape[0]):
            q_pair = q_ref[r, :, sl]
            k_pair = k_ref[r, :, sl]
            s_old = state[r, pair]
            s_new = s_old * cdec_ref[pair]
            for half in range(2):
                h = 2 * pair + half
                vsl = slice(h * B_VDIM, (h + 1) * B_VDIM)
                v = v_ref[r, :, vsl]
                q = _keep_head(q_pair, half, low_half)
                inner = lax.dot_general(q, k_pair, _NT, preferred_element_type=F32) * dmat_ref[h]
                y = jnp.dot(inner.astype(BF16), v, preferred_element_type=F32)
                y = y + jnp.dot(q, s_old.astype(BF16), preferred_element_type=F32) * qdec_ref[h]
                kd = _keep_head((k_pair.astype(F32) * kdec_ref[h]).astype(BF16), half, low_half)
                s_new = s_new + lax.dot_general(kd, v, _TN, preferred_element_type=F32)
                o_ref[r, :, vsl] = (_rms(y) * jax.nn.silu(g_ref[r, :, vsl])).astype(BF16)
            state[r, pair] = s_new


def _retention(q, k, v, gate):
    b, lp, _ = q.shape
    t = RET_TILE
    log_g = jnp.log(1.0 - 2.0 ** (-5.0 - jnp.arange(B_HEADS, dtype=F32)))
    idx = jnp.arange(t, dtype=F32)
    rel = idx[:, None] - idx[None, :]
    dmat = jnp.where(rel >= 0, jnp.exp(jnp.maximum(rel, 0.0)[None] * log_g[:, None, None]), 0.0)
    qdec = jnp.exp((idx + 1.0)[None, :, None] * log_g[:, None, None])
    kdec = jnp.exp((t - 1.0 - idx)[None, :, None] * log_g[:, None, None])
    cdec = jnp.repeat(jnp.exp(t * log_g), HEAD_DIM).reshape(B_HEADS // 2, LANES, 1)

    qk_w, v_w = B_HEADS * HEAD_DIM, B_HEADS * B_VDIM
    nb = MIX_BATCH
    seq = lambda n: pl.BlockSpec((nb, t, n), lambda bi, i: (bi, i, 0))
    return pl.pallas_call(
        _retention_kernel,
        grid=(b // nb, lp // t),
        in_specs=[seq(qk_w), seq(qk_w), seq(v_w), seq(v_w),
                  _resident((B_HEADS, t, t)), _resident((B_HEADS, t, 1)),
                  _resident((B_HEADS, t, 1)), _resident((B_HEADS // 2, LANES, 1))],
        out_specs=seq(v_w),
        out_shape=jax.ShapeDtypeStruct((b, lp, v_w), BF16),
        scratch_shapes=[pltpu.VMEM((nb, B_HEADS // 2, LANES, B_VDIM), F32)],
        compiler_params=_params(("parallel", "arbitrary")),
        name="retention",
    )(q, k, v, gate, dmat, qdec, kdec, cdec)


def _swa_kernel(q_ref, k_ref, v_ref, sink_ref, o_ref):
    j = pl.program_id(1)
    nb, tq = q_ref.shape[0], q_ref.shape[1]
    lp = k_ref.shape[1]
    win = tq + C_WIN_CHUNKS * CHUNK
    rep = C_HEADS // C_KV_HEADS
    low_half = _low_half()

    start = pl.multiple_of(jnp.clip(j * tq - C_WIN_CHUNKS * CHUNK, 0, lp - win), CHUNK)
    q_row = j * tq + lax.broadcasted_iota(jnp.int32, (tq, 1), 0)
    q_col = j * tq + lax.broadcasted_iota(jnp.int32, (1, tq), 1)
    q_chunk = jnp.concatenate([q_col // CHUNK] * rep, axis=1)
    key = lax.broadcasted_iota(jnp.int32, (CHUNK + win, 1), 0)
    k_chunk = start // CHUNK + (key - CHUNK) // CHUNK
    in_window = ((key >= CHUNK) & (k_chunk >= 1) & (k_chunk <= q_chunk)
                 & (k_chunk >= q_chunk - C_WIN_CHUNKS))
    valid = in_window | ((key < CHUNK) & (key >= PAD))
    v_row_low = lax.broadcasted_iota(jnp.int32, (LANES, 1), 0) < HEAD_DIM

    for r in range(nb):
        q = jnp.where(q_row < lp, q_ref[r], jnp.zeros_like(q_ref[r]))
        for g in range(C_KV_HEADS):
            sl = slice(g * LANES, (g + 1) * LANES)
            keys = jnp.concatenate([k_ref[r, 0:CHUNK, sl], k_ref[r, pl.ds(start, win), sl]], axis=0)
            vals = jnp.concatenate([v_ref[r, 0:CHUNK, sl], v_ref[r, pl.ds(start, win), sl]], axis=0)
            heads = range(g * rep, (g + 1) * rep)
            qs = jnp.concatenate(
                [_keep_head(q[:, (h // 2) * LANES:(h // 2 + 1) * LANES], h % 2, low_half)
                 for h in heads], axis=0)
            sink = jnp.concatenate(
                [jnp.broadcast_to(sink_ref[0:1, h:h + 1] * LOG2E, (1, tq)) for h in heads], axis=1)
            s = jnp.where(valid, lax.dot_general(keys, qs, _NT, preferred_element_type=F32), NEG)
            m = jnp.maximum(jnp.max(s, axis=0, keepdims=True), sink)
            p = jnp.exp2(s - m)
            denom = jnp.sum(p, axis=0, keepdims=True) + jnp.exp2(sink - m)
            y_t = lax.dot_general(vals, p.astype(BF16), _TN, preferred_element_type=F32) / denom
            for u in range(rep // 2):
                out_t = jnp.where(v_row_low, y_t[:, 2 * u * tq:(2 * u + 1) * tq],
                                  y_t[:, (2 * u + 1) * tq:(2 * u + 2) * tq])
                pair = g * rep // 2 + u
                o_ref[r, :, pair * LANES:(pair + 1) * LANES] = jnp.where(
                    q_row >= PAD, out_t.T, 0.0).astype(BF16)


def _swa(q, k, v, sinks):
    b, lp, width = q.shape
    tq = KEY_TILE
    nb = MIX_BATCH
    kv = pl.BlockSpec((nb, lp, k.shape[2]), lambda bi, j: (bi, 0, 0))
    return pl.pallas_call(
        _swa_kernel,
        grid=(b // nb, -(-lp // tq)),
        in_specs=[pl.BlockSpec((nb, tq, width), lambda bi, j: (bi, j, 0)), kv, kv,
                  _resident((1, C_HEADS))],
        out_specs=pl.BlockSpec((nb, tq, width), lambda bi, j: (bi, j, 0)),
        out_shape=jax.ShapeDtypeStruct((b, lp, width), BF16),
        compiler_params=_params(("parallel", "parallel")),
        name="swa_sinks",
    )(q, k, v, sinks.reshape(1, C_HEADS))


def _pool_kernel(x_ref, mix_ref, scale_ref, o_ref):
    lp = x_ref.shape[1]
    row = lax.broadcasted_iota(jnp.int32, (lp, 1), 0)
    t = row - PAD
    outs = []
    for gi, w in enumerate(POOL_WINDOWS):
        x = x_ref[0, :, gi * D_GROUP:(gi + 1) * D_GROUP]
        s, shift = x, 1
        while shift < w:
            s = s + jnp.where(row >= shift, pltpu.roll(s, shift, 0), 0.0)
            shift *= 2
        count = jnp.maximum(jnp.minimum(t + 1, w), 1).astype(F32)
        y = (s / count - x).astype(BF16)
        outs.append(jnp.dot(y, mix_ref[gi].astype(BF16), preferred_element_type=F32))
    out = jnp.concatenate(outs, axis=-1) * scale_ref[...]
    o_ref[0] = jnp.where(row >= PAD, out, 0.0).astype(BF16)


def _pool(x, mix, scale):
    b, lp, width = x.shape
    n_groups = len(POOL_WINDOWS)
    return pl.pallas_call(
        _pool_kernel,
        grid=(b,),
        in_specs=[pl.BlockSpec((1, lp, width), lambda bi: (bi, 0, 0)),
                  _resident((n_groups, D_GROUP, D_GROUP)), _resident((1, width))],
        out_specs=pl.BlockSpec((1, lp, width), lambda bi: (bi, 0, 0)),
        out_shape=jax.ShapeDtypeStruct((b, lp, width), BF16),
        compiler_params=_params(("parallel",)),
        name="pool_mixer",
    )(x, mix, scale.reshape(1, width))


def kernel(x, meta_tokens, ffn1_norm, ffn1_w_in, ffn1_w_out, mix_norm, ffn2_norm, ffn2_w_in,
           ffn2_w_out, ev_w_in, ev_a_q_norm, ev_a_k_norm, ev_w_out, od_w_in, od_c_q_norm,
           od_c_k_norm, od_c_sinks, od_d_mix, od_d_scale, od_w_out):
    b, s, d = x.shape
    lp = PAD + N_META + s
    assert d == D_MODEL and (b * lp) % ROW_TILE == 0 and s % ROW_TILE == 0
    assert lp % PROJ_TILE == 0 and lp % RET_TILE == 0
    assert min(TOP_K, s // 4) == TOP_K and b % DSA_BATCH == 0 and b % MIX_BATCH == 0
    depth = ffn1_norm.shape[0]
    seq = lambda a: a.reshape(b, lp, a.shape[-1])
    flat = lambda a: a.reshape(b * lp, a.shape[-1])

    ffn1_w_in, ffn1_w_out, ffn2_w_in, ffn2_w_out = (
        w.astype(BF16) for w in (ffn1_w_in, ffn1_w_out, ffn2_w_in, ffn2_w_out))
    for layer in range(depth):
        if layer == 0:
            h = _ffn_first(x, meta_tokens, ffn1_norm[0], ffn1_w_in, ffn1_w_out, 0)
        else:
            h = _ffn(h, ffn1_norm[layer], ffn1_w_in, ffn1_w_out, layer)
        if layer % 2 == 0:
            e = layer // 2
            aq, ak, av, iq, ik, iw, bq, bk, bv, bg = _even_proj(
                seq(h), mix_norm[layer], ev_w_in[e], ev_a_q_norm[e], ev_a_k_norm[e])
            ya = _dsa(aq, ak, av, iq, ik, iw)
            yb = _retention(bq, bk, bv, bg)
            mix = (flat(ya), flat(yb), ev_w_out[e].astype(BF16))
        else:
            o = layer // 2
            cq, ck, cv, dx = _odd_proj(seq(h), mix_norm[layer], od_w_in[o], od_c_q_norm[o],
                                       od_c_k_norm[o])
            yc = _swa(cq, ck, cv, od_c_sinks[o])
            yd = _pool(dx, od_d_mix[o], od_d_scale[o])
            mix = (flat(yc), flat(yd), od_w_out[o].astype(BF16))
        last = layer == depth - 1
        h = _ffn(h, ffn2_norm[layer], ffn2_w_in, ffn2_w_out, layer, (b, s) if last else None, mix)
    return h
```

```python
import functools

import jax
import jax.numpy as jnp
import numpy as np
from jax import lax
from jax.experimental import pallas as pl
from jax.experimental.pallas import tpu as pltpu

F32 = jnp.float32
BF16 = jnp.bfloat16

D_MODEL = 1024
D_FF = 2816
CHUNK = 64
N_META = 16
PAD = CHUNK - N_META
HEAD_DIM = 64
LANES = 128
A_HEADS = 8
IDX_HEADS = 8
IDX_DIM = 64
TOP_K = 256
B_HEADS = 8
B_VDIM = 128
C_HEADS = 8
C_KV_HEADS = 2
C_WIN_CHUNKS = 2
D_WIDTH = 512
POOL_WINDOWS = (2, 4, 8, 16)
D_GROUP = 128
EPS = 1e-6
NEG = -1e30
BIG = 3e38
LOG2E = 1.4426950408889634

VMEM_LIMIT = 56 * 1024 * 1024
ROW_TILE = 512
FF_CHUNK = 256
PROJ_TILE = 704
KEY_TILE = 128
KEY_UNROLL = 4
DSA_BATCH = 2
MIX_BATCH = 2
SWA_BATCH = 4
RET_TILE = 528
BISECT_ITERS = 15

_NT = (((1,), (1,)), ((), ()))
_TN = (((0,), (0,)), ((), ()))


def _params(sem):
    return pltpu.CompilerParams(dimension_semantics=sem, vmem_limit_bytes=VMEM_LIMIT)


def _rms(x, gain=None):
    y = x * lax.rsqrt(jnp.mean(x * x, axis=-1, keepdims=True) + EPS)
    return y if gain is None else y * gain


def _resident(shape):
    return pl.BlockSpec(shape, lambda *_: (0,) * len(shape), pipeline_mode=pl.Buffered(1))


def _low_half():
    return lax.broadcasted_iota(jnp.int32, (1, LANES), 1) < HEAD_DIM


def _keep_head(x, half, low_half):
    return jnp.where(low_half if half == 0 else jnp.logical_not(low_half), x, jnp.zeros_like(x))


def _ffn_kernel(h_ref, g_ref, win_ref, wout_ref, o_ref, mix_refs=None):
    x = h_ref[...]
    if mix_refs is not None:
        ya_ref, yb_ref, wmix_ref = mix_refs
        na = ya_ref.shape[1]
        x = x + jnp.dot(ya_ref[...], wmix_ref[0:na, :], preferred_element_type=F32)
        x = x + jnp.dot(yb_ref[...], wmix_ref[na:, :], preferred_element_type=F32)
    xn = _rms(x, g_ref[...]).astype(BF16)
    acc = jnp.zeros(x.shape, F32)
    for c in range(D_FF // FF_CHUNK):
        lo = c * FF_CHUNK
        g = jnp.dot(xn, win_ref[:, lo:lo + FF_CHUNK], preferred_element_type=F32)
        u = jnp.dot(xn, win_ref[:, D_FF + lo:D_FF + lo + FF_CHUNK], preferred_element_type=F32)
        a = (jax.nn.silu(g) * u).astype(BF16)
        acc = acc + jnp.dot(a, wout_ref[lo:lo + FF_CHUNK, :], preferred_element_type=F32)
    o_ref[...] = x + 0.5 * acc


def _mix_ffn_kernel(h_ref, ya_ref, yb_ref, wmix_ref, g_ref, win_ref, wout_ref, o_ref):
    _ffn_kernel(h_ref, g_ref, win_ref, wout_ref, o_ref, (ya_ref, yb_ref, wmix_ref))


def _layer_weights(shape, layer):
    return pl.BlockSpec((None,) + shape, lambda *_: (layer, 0, 0), pipeline_mode=pl.Buffered(1))


def _ffn_weight_specs(layer):
    return [_resident((1, D_MODEL)), _layer_weights((D_MODEL, 2 * D_FF), layer),
            _layer_weights((D_FF, D_MODEL), layer)]


def _frame_rows(b, lp, s, n):
    return pl.BlockSpec(
        (pl.Element(ROW_TILE), pl.Element(n)),
        lambda bi, t: (pl.multiple_of(bi * lp + lp - s + ROW_TILE * t, CHUNK), 0))


def _ffn(h, gain, w_in, w_out, layer, frames_out=None, mix=None):
    if frames_out is None:
        rows = h.shape[0]
        grid = (rows // ROW_TILE,)
        row_spec = lambda n: pl.BlockSpec((ROW_TILE, n), lambda i: (i, 0))
        out_spec = row_spec(D_MODEL)
        out_shape = jax.ShapeDtypeStruct((rows, D_MODEL), F32)
        sem = ("parallel",)
    else:
        b, s = frames_out
        grid = (b, s // ROW_TILE)
        row_spec = functools.partial(_frame_rows, b, h.shape[0] // b, s)
        out_spec = pl.BlockSpec((None, ROW_TILE, D_MODEL), lambda bi, t: (bi, t, 0))
        out_shape = jax.ShapeDtypeStruct((b, s, D_MODEL), F32)
        sem = ("parallel", "parallel")
    if mix is None:
        body, rows_in, specs = _ffn_kernel, [h], [row_spec(D_MODEL)]
    else:
        ya, yb, w_mix = mix
        body, rows_in = _mix_ffn_kernel, [h, ya, yb, w_mix]
        specs = [row_spec(D_MODEL), row_spec(ya.shape[1]), row_spec(yb.shape[1]),
                 _resident(w_mix.shape)]
    return pl.pallas_call(
        body,
        grid=grid,
        in_specs=specs + _ffn_weight_specs(layer),
        out_specs=out_spec,
        out_shape=out_shape,
        compiler_params=_params(sem),
        name="ffn",
    )(*rows_in, gain.reshape(1, D_MODEL), w_in, w_out)


def _ffn_meta_kernel(c_ref, g_ref, win_ref, wout_ref, h_hbm_ref, o_ref, y_s):
    @pl.when(pl.program_id(0) == 0)
    def _():
        _ffn_kernel(c_ref, g_ref, win_ref, wout_ref, y_s)

    o_ref[...] = y_s[...]


def _ffn_first(x, meta_tokens, gain, w_in, w_out, layer):
    b, s, _ = x.shape
    lp = PAD + N_META + s
    tiles = s // ROW_TILE
    gain = gain.reshape(1, D_MODEL)
    h = pl.pallas_call(
        _ffn_kernel,
        grid=(b, tiles),
        in_specs=[pl.BlockSpec((ROW_TILE, D_MODEL), lambda bi, t: (bi * tiles + t, 0))]
        + _ffn_weight_specs(layer),
        out_specs=_frame_rows(b, lp, s, D_MODEL),
        out_shape=jax.ShapeDtypeStruct((b * lp, D_MODEL), F32),
        compiler_params=_params(("parallel", "parallel")),
        name="ffn_frames",
    )(x.reshape(b * s, D_MODEL), gain, w_in, w_out)
    chunk0 = jnp.concatenate([jnp.zeros((PAD, D_MODEL), x.dtype), meta_tokens.astype(x.dtype)], axis=0)
    return pl.pallas_call(
        _ffn_meta_kernel,
        grid=(b,),
        in_specs=[_resident((CHUNK, D_MODEL))] + _ffn_weight_specs(layer)
        + [pl.BlockSpec(memory_space=pl.ANY)],
        out_specs=pl.BlockSpec((CHUNK, D_MODEL), lambda bi: (bi * (lp // CHUNK), 0)),
        out_shape=jax.ShapeDtypeStruct((b * lp, D_MODEL), F32),
        scratch_shapes=[pltpu.VMEM((CHUNK, D_MODEL), F32)],
        input_output_aliases={4: 0},
        compiler_params=_params(("arbitrary",)),
        name="ffn_meta",
    )(chunk0, gain, w_in, w_out, h)


def _head_norm(y, seg_ref, gain):
    sq = (y * y).astype(BF16)
    ms = jnp.concatenate(
        [jnp.dot(sq[:, p * LANES:(p + 1) * LANES], seg_ref[...], preferred_element_type=F32)
         for p in range(y.shape[1] // LANES)], axis=-1)
    return y * lax.rsqrt(ms + EPS) * gain


def _rotate_half(x, first_half):
    n = x.shape[-1]
    half = HEAD_DIM // 2
    return jnp.where(first_half, -pltpu.roll(x, n - half, 1), pltpu.roll(x, half, 1))


def _column_groups(xn, w_refs):
    state = [0, 0]

    def cols(n):
        if state[1] == w_refs[state[0]].shape[1]:
            state[:] = [state[0] + 1, 0]
        w_ref, off = w_refs[state[0]], state[1]
        state[1] += n
        return jnp.dot(xn, w_ref[:, off:off + n], preferred_element_type=F32)

    return cols


def _even_proj_kernel(h_ref, g_ref, wa_ref, wi_ref, wb_ref, seg_ref, qg_ref, kg_ref, cos_ref, sin_ref,
                      aq_ref, ak_ref, av_ref, iq_ref, ik_ref, iw_ref, bq_ref, bk_ref, bv_ref,
                      bg_ref):
    xn = _rms(h_ref[0], g_ref[...]).astype(BF16)
    cols = _column_groups(xn, (wa_ref, wi_ref, wb_ref))

    w = A_HEADS * HEAD_DIM
    aq_ref[0] = _head_norm(cols(w), seg_ref, qg_ref[...]).astype(BF16)
    ak_ref[0] = _head_norm(cols(w), seg_ref, kg_ref[...]).astype(BF16)
    av_ref[0] = cols(w).astype(BF16)
    iq_ref[0] = cols(IDX_HEADS * IDX_DIM).astype(BF16)
    ik_ref[0] = cols(LANES).astype(BF16)
    iw_ref[0] = cols(LANES) * (IDX_HEADS ** -0.5 * IDX_DIM ** -0.5)
    cos, sin = cos_ref[...], sin_ref[...]
    first_half = lax.broadcasted_iota(jnp.int32, (1, w), 1) % HEAD_DIM < HEAD_DIM // 2
    q = cols(w)
    bq_ref[0] = (q * cos + _rotate_half(q, first_half) * sin).astype(BF16)
    k = cols(w)
    bk_ref[0] = ((k * cos + _rotate_half(k, first_half) * sin) * HEAD_DIM ** -0.5).astype(BF16)
    for c in range(2):
        sl = slice(c * w, (c + 1) * w)
        bv_ref[0, :, sl] = cols(w).astype(BF16)
    for c in range(2):
        sl = slice(c * w, (c + 1) * w)
        bg_ref[0, :, sl] = cols(w)


def _odd_proj_kernel(h_ref, g_ref, w_ref, seg_ref, qg_ref, kg_ref, cq_ref, ck_ref, cv_ref, dx_ref):
    xn = _rms(h_ref[0], g_ref[...]).astype(BF16)
    cols = _column_groups(xn, (w_ref,))

    kv_w = 2 * C_KV_HEADS * HEAD_DIM
    cq_ref[0] = _head_norm(cols(C_HEADS * HEAD_DIM), seg_ref, qg_ref[...]).astype(BF16)
    ck_ref[0] = _head_norm(cols(kv_w), seg_ref, kg_ref[...]).astype(BF16)
    cv_ref[0] = cols(kv_w).astype(BF16)
    dx_ref[0] = cols(D_WIDTH)


def _seg_matrix():
    g = jnp.arange(LANES) // HEAD_DIM
    return jnp.where(g[:, None] == g[None, :], 1.0 / HEAD_DIM, 0.0).astype(BF16)


def _proj_call(body, name, h, gain, weights, extra_in, extra_specs, outs):
    b, lp, _ = h.shape
    t = PROJ_TILE
    seq = lambda n: pl.BlockSpec((1, t, n), lambda bi, i: (bi, i, 0))
    return pl.pallas_call(
        body,
        grid=(b, lp // t),
        in_specs=[seq(D_MODEL), _resident((1, D_MODEL))]
        + [_resident((w.shape[0], n)) for w, n in weights]
        + [_resident((LANES, LANES))] + extra_specs,
        out_specs=[seq(n) for n, _ in outs],
        out_shape=[jax.ShapeDtypeStruct((b, lp, n), dt) for n, dt in outs],
        compiler_params=_params(("parallel", "parallel")),
        name=name,
    )(h, gain.reshape(1, D_MODEL), *[w for w, _ in weights], _seg_matrix(), *extra_in)


def _rotary_tables(lp):
    pos = np.arange(lp, dtype=np.float64) - PAD
    inv = 1.0 / (10000.0 ** (np.arange(0, HEAD_DIM, 2, dtype=np.float64) / HEAD_DIM))
    ang = pos[:, None] * inv[None]
    return (jnp.asarray(np.tile(np.cos(ang), (1, 2 * B_HEADS)), F32),
            jnp.asarray(np.tile(np.sin(ang), (1, 2 * B_HEADS)), F32))


def _even_proj(h, gain, w, q_gain, k_gain):
    lp = h.shape[1]
    w = w.astype(BF16)
    a_w = 3 * A_HEADS * HEAD_DIM + IDX_HEADS * IDX_DIM
    ik = w[:, a_w:a_w + IDX_DIM]
    iw = jnp.pad(w[:, a_w + IDX_DIM:a_w + IDX_DIM + IDX_HEADS], ((0, 0), (0, LANES - IDX_HEADS)))
    weights = [(w, a_w), (jnp.concatenate([ik, ik, iw], axis=1), 2 * LANES),
               (w[:, a_w + IDX_DIM + IDX_HEADS:], w.shape[1] - a_w - IDX_DIM - IDX_HEADS)]
    cos, sin = _rotary_tables(lp)
    w_heads = A_HEADS * HEAD_DIM
    qg = jnp.tile(q_gain, A_HEADS).reshape(1, w_heads) * (HEAD_DIM ** -0.5 * LOG2E)
    kg = jnp.tile(k_gain, A_HEADS).reshape(1, w_heads)
    table = pl.BlockSpec((PROJ_TILE, w_heads), lambda bi, i: (i, 0))
    outs = [(w_heads, BF16)] * 4 + [(LANES, BF16), (LANES, F32), (w_heads, BF16), (w_heads, BF16),
                                    (B_HEADS * B_VDIM, BF16), (B_HEADS * B_VDIM, F32)]
    return _proj_call(_even_proj_kernel, "even_in_proj", h, gain, weights,
                      [qg, kg, cos, sin],
                      [_resident((1, w_heads)), _resident((1, w_heads)), table, table], outs)


def _odd_proj(h, gain, w, q_gain, k_gain):
    w = w.astype(BF16)
    q_w = C_HEADS * HEAD_DIM
    kv_w = C_KV_HEADS * HEAD_DIM
    dup = lambda m: jnp.concatenate(
        [m[:, g * HEAD_DIM:(g + 1) * HEAD_DIM] for g in range(C_KV_HEADS) for _ in range(2)], axis=1)
    w = jnp.concatenate([w[:, :q_w], dup(w[:, q_w:q_w + kv_w]), dup(w[:, q_w + kv_w:q_w + 2 * kv_w]),
                         w[:, q_w + 2 * kv_w:]], axis=1)
    qg = jnp.tile(q_gain, C_HEADS).reshape(1, q_w) * (HEAD_DIM ** -0.5 * LOG2E)
    kg = jnp.tile(k_gain, 2 * C_KV_HEADS).reshape(1, 2 * kv_w)
    outs = [(q_w, BF16), (2 * kv_w, BF16), (2 * kv_w, BF16), (D_WIDTH, F32)]
    return _proj_call(_odd_proj_kernel, "odd_in_proj", h, gain, [(w, w.shape[1])], [qg, kg],
                      [_resident((1, q_w)), _resident((1, 2 * kv_w))], outs)


def _fold(x):
    return x.reshape(x.shape[0] // 8, 8, x.shape[1]).sum(axis=0)


def _any(mask):
    return jnp.max(jnp.where(mask, 1.0, 0.0)) > 0.0


def _topk_threshold(sc_s, n_groups, lo, hi, search):
    nq = sc_s.shape[2]

    def scan(fn, init):
        def body(g, acc):
            for u in range(KEY_UNROLL):
                c = g * KEY_UNROLL + u
                acc = fn(acc, sc_s[c], c)
            return acc
        return lax.fori_loop(0, n_groups, body, init)

    def count(pred):
        acc = scan(lambda a, s, c: a + _fold(jnp.where(pred(s, c), 1.0, 0.0)), jnp.zeros((8, nq), F32))
        return jnp.sum(acc, axis=0, keepdims=True)

    def step(lo, hi, mid):
        up = count(lambda s, c: s > mid) >= TOP_K
        return jnp.where(up, mid, lo), jnp.where(up, hi, mid)

    def coarse(_, c):
        lo, hi = c
        return step(lo, hi, 0.5 * (lo + hi))

    lo, hi = lax.fori_loop(0, BISECT_ITERS, coarse, (lo, hi))

    def bounds(lo, hi):
        def fn(acc, s, c):
            t_lo, t_hi = acc
            t_lo = jnp.minimum(t_lo, jnp.min(jnp.where(s > lo, s, BIG), axis=0, keepdims=True))
            t_hi = jnp.maximum(t_hi, jnp.max(jnp.where(s <= hi, s, -BIG), axis=0, keepdims=True))
            return t_lo, t_hi
        return scan(fn, (jnp.full((1, nq), BIG, F32), jnp.full((1, nq), -BIG, F32)))

    def unfinished(c):
        _, _, t_lo, t_hi = c
        return _any((t_lo != t_hi) & search)

    def refine(c):
        lo, hi, t_lo, t_hi = c
        mid = 0.5 * (t_lo + t_hi)
        mid = jnp.where(mid >= t_hi, t_lo, mid)
        lo, hi = step(lo, hi, mid)
        return (lo, hi) + bounds(lo, hi)

    _, _, _, thr = lax.while_loop(unfinished, refine, (lo, hi) + bounds(lo, hi))
    return thr, TOP_K - count(lambda s, c: s > thr)


def _dsa_kernel(aq_ref, iq_ref, iw_ref, ak_ref, av_ref, ik_ref, o_ref, k_s, vt_s, ik_s, sc_s, acc_s):
    j = pl.program_id(1)
    nb = aq_ref.shape[0]
    lp = ak_ref.shape[1]
    kt = KEY_TILE
    n_full, tail = divmod(lp, kt)
    pairs = A_HEADS // 2
    rows = range(nb)

    @pl.when(j == 0)
    def _():
        def put(r, c, k, ik, v):
            k_s[r, c] = k
            ik_s[r, c] = ik
            v = v.astype(F32)
            for p in range(pairs):
                vt_s[r, c, p] = v[:, p * LANES:(p + 1) * LANES].T.astype(BF16)

        def copy_tile(c, carry):
            at = pl.ds(pl.multiple_of(c * kt, kt), kt)
            for r in rows:
                put(r, c, ak_ref[r, at, :], ik_ref[r, at, :], av_ref[r, at, :])
            return carry

        lax.fori_loop(0, n_full, copy_tile, 0)
        padded = lambda a: jnp.concatenate([a, jnp.zeros((kt - tail, a.shape[1]), a.dtype)], axis=0)
        at = slice(n_full * kt, lp)
        zeros = lambda ref: jnp.zeros(ref.shape[2:], BF16)
        for r in rows:
            if tail:
                put(r, n_full, padded(ak_ref[r, at, :]), padded(ik_ref[r, at, :]),
                    padded(av_ref[r, at, :]))
            for c in range(n_full + (tail > 0), k_s.shape[1]):
                put(r, c, zeros(k_s), zeros(ik_s), zeros(k_s))
        sc_s[...] = jnp.full(sc_s.shape, NEG, F32)

    low_half = _low_half()
    q_row = j * kt + lax.broadcasted_iota(jnp.int32, (kt, 1), 0)
    q_col = j * kt + lax.broadcasted_iota(jnp.int32, (1, kt), 1)
    in_range = q_row < lp

    def both_heads(x, p):
        x = x[:, p * LANES:(p + 1) * LANES]
        return jnp.concatenate([_keep_head(x, 0, low_half), _keep_head(x, 1, low_half)], axis=0)

    aq_p, iq_p, iw_t = [], [], []
    for r in rows:
        aq = jnp.where(in_range, aq_ref[r], jnp.zeros_like(aq_ref[r]))
        iq = jnp.where(in_range, iq_ref[r], jnp.zeros_like(iq_ref[r]))
        aq_p.append([both_heads(aq, p) for p in range(pairs)])
        iq_p.append([both_heads(iq, p) for p in range(pairs)])
        iw_t.append(jnp.where(in_range, iw_ref[r], 0.0).T)
    key_grid = lax.broadcasted_iota(jnp.int32, (kt, kt), 0)
    key_limit = jnp.minimum((q_col // CHUNK + 1) * CHUNK, lp)
    n_groups = (j + KEY_UNROLL) // KEY_UNROLL
    cols = lambda r: slice(r * kt, (r + 1) * kt)

    def score_tiles(g, carry):
        mn, mx = carry
        tiles = [g * KEY_UNROLL + u for u in range(KEY_UNROLL)]
        logits = [[[lax.dot_general(ik_s[r, c], iq_p[r][p], _NT, preferred_element_type=F32)
                    for p in range(pairs)] for r in rows] for c in tiles]
        mn, mx = list(mn), list(mx)
        for c, per_row in zip(tiles, logits):
            key = key_grid + c * kt
            admissible = (key >= jnp.where(c == 0, PAD, 0)) & (key < key_limit)
            for r, lg in zip(rows, per_row):
                sc = jnp.zeros((kt, kt), F32)
                for p in range(pairs):
                    for half in range(2):
                        h = 2 * p + half
                        sc = sc + iw_t[r][h:h + 1, :] * jnp.maximum(
                            lg[p][:, half * kt:(half + 1) * kt], 0.0)
                sc_s[c, :, cols(r)] = jnp.where(admissible, sc, NEG)
                mn[r] = jnp.minimum(mn[r], jnp.min(jnp.where(admissible, sc, BIG), axis=0, keepdims=True))
                mx[r] = jnp.maximum(mx[r], jnp.max(jnp.where(admissible, sc, NEG), axis=0, keepdims=True))
        return tuple(mn), tuple(mx)

    mn, mx = lax.fori_loop(0, n_groups, score_tiles,
                           (tuple(jnp.full((1, kt), BIG, F32) for _ in rows),
                            tuple(jnp.full((1, kt), NEG, F32) for _ in rows)))
    mn = jnp.concatenate(mn, axis=1)
    mx = jnp.concatenate(mx, axis=1)

    first_search_block = (TOP_K - N_META + CHUNK - 1) // CHUNK * CHUNK // kt

    tri = (lax.broadcasted_iota(jnp.int32, (kt, kt), 0)
           >= lax.broadcasted_iota(jnp.int32, (kt, kt), 1)).astype(BF16)

    def to_bias(thr, n_equal):
        def body(g, seen):
            for u in range(KEY_UNROLL):
                c = g * KEY_UNROLL + u
                s = sc_s[c]
                equal = s == thr
                rank = seen + jnp.dot(tri, jnp.where(equal, 1.0, 0.0).astype(BF16),
                                      preferred_element_type=F32)
                sc_s[c] = jnp.where((s > thr) | (equal & (rank <= n_equal)), 0.0, NEG)
                seen = rank[kt - 1:kt, :]
            return seen
        lax.fori_loop(0, n_groups, body, jnp.zeros((1, nb * kt), F32))

    @pl.when(j < first_search_block)
    def _():
        to_bias(jnp.full((1, nb * kt), 0.5 * NEG, F32), jnp.zeros((1, nb * kt), F32))

    @pl.when(j >= first_search_block)
    def _():
        search = jnp.concatenate([q_col < lp] * nb, axis=1)
        to_bias(*_topk_threshold(sc_s, n_groups, mn - (jnp.abs(mn) * 2.0 ** -10 + 1e-30), mx, search))

    acc_s[...] = jnp.zeros(acc_s.shape, F32)
    chains = [(r, p) for r in rows for p in range(pairs)]
    ones = jnp.ones((8, KEY_UNROLL * kt), BF16)

    def attend(g, carry):
        m, l = carry
        c0 = g * KEY_UNROLL
        s = []
        for r in rows:
            bias = jnp.concatenate([sc_s[c0 + u, :, cols(r)] for u in range(KEY_UNROLL)], axis=0)
            bias = jnp.concatenate([bias, bias], axis=1)
            for p in range(pairs):
                k = jnp.concatenate([k_s[r, c0 + u, :, p * LANES:(p + 1) * LANES]
                                     for u in range(KEY_UNROLL)], axis=0)
                s.append(lax.dot_general(k, aq_p[r][p], _NT, preferred_element_type=F32) + bias)
        new_m, new_l = [], []
        for i, (r, p) in enumerate(chains):
            m_i = jnp.maximum(m[i], jnp.max(s[i], axis=0, keepdims=True))
            alpha = jnp.exp2(m[i] - m_i)
            pr = jnp.exp2((s[i] - m_i).astype(BF16))
            new_l.append(alpha * l[i] + jnp.dot(ones, pr, preferred_element_type=F32)[0:1])
            vt = jnp.concatenate([vt_s[r, c0 + u, p] for u in range(KEY_UNROLL)], axis=1)
            acc_s[r, p] = alpha * acc_s[r, p] + jnp.dot(vt, pr, preferred_element_type=F32)
            new_m.append(m_i)
        return tuple(new_m), tuple(new_l)

    init = (tuple(jnp.full((1, 2 * kt), -BIG, F32) for _ in chains),
            tuple(jnp.zeros((1, 2 * kt), F32) for _ in chains))
    _, l = lax.fori_loop(0, n_groups, attend, init)

    v_row_low = lax.broadcasted_iota(jnp.int32, (LANES, 1), 0) < HEAD_DIM
    for i, (r, p) in enumerate(chains):
        out_t = acc_s[r, p] / l[i]
        out_t = jnp.where(v_row_low, out_t[:, 0:kt], out_t[:, kt:2 * kt])
        o_ref[r, :, p * LANES:(p + 1) * LANES] = jnp.where(q_row >= PAD, out_t.T, 0.0).astype(BF16)


def _dsa(aq, ak, av, iq, ik, iw):
    b, lp, width = aq.shape
    kt = KEY_TILE
    nb = DSA_BATCH
    n_tiles = -(-lp // kt)
    n_scan = -(-n_tiles // KEY_UNROLL) * KEY_UNROLL
    q_spec = lambda n: pl.BlockSpec((nb, kt, n), lambda bi, j: (bi, j, 0))
    full_spec = lambda n: pl.BlockSpec((nb, lp, n), lambda bi, j: (bi, 0, 0))
    return pl.pallas_call(
        _dsa_kernel,
        grid=(b // nb, n_tiles),
        in_specs=[q_spec(width), q_spec(width), q_spec(LANES), full_spec(width), full_spec(width),
                  full_spec(LANES)],
        out_specs=q_spec(width),
        out_shape=jax.ShapeDtypeStruct((b, lp, width), BF16),
        scratch_shapes=[
            pltpu.VMEM((nb, n_scan, kt, width), BF16),
            pltpu.VMEM((nb, n_scan, A_HEADS // 2, LANES, kt), BF16),
            pltpu.VMEM((nb, n_scan, kt, LANES), BF16),
            pltpu.VMEM((n_scan, kt, nb * kt), F32),
            pltpu.VMEM((nb, A_HEADS // 2, LANES, 2 * kt), F32),
        ],
        compiler_params=_params(("parallel", "arbitrary")),
        name="dsa_attention",
    )(aq, iq, iw, ak, av, ik)


def _retention_kernel(q_ref, k_ref, v_ref, g_ref, dmat_ref, qdec_ref, kdec_ref, cdec_ref, o_ref,
                      state):
    i = pl.program_id(1)

    @pl.when(i == 0)
    def _():
        state[...] = jnp.zeros(state.shape, F32)

    low_half = _low_half()
    for pair in range(B_HEADS // 2):
        sl = slice(pair * LANES, (pair + 1) * LANES)
        for r in range(q_ref.shape[0]):
            q_pair = q_ref[r, :, sl]
            k_pair = k_ref[r, :, sl]
            s_old = state[r, pair]
            s_new = s_old * cdec_ref[pair]
            for half in range(2):
                h = 2 * pair + half
                vsl = slice(h * B_VDIM, (h + 1) * B_VDIM)
                v = v_ref[r, :, vsl]
                q = _keep_head(q_pair, half, low_half)
                inner = lax.dot_general(q, k_pair, _NT, preferred_element_type=F32) * dmat_ref[h]
                y = jnp.dot(inner.astype(BF16), v, preferred_element_type=F32)
                y = y + jnp.dot(q, s_old.astype(BF16), preferred_element_type=F32) * qdec_ref[h]
                kd = _keep_head((k_pair.astype(F32) * kdec_ref[h]).astype(BF16), half, low_half)
                s_new = s_new + lax.dot_general(kd, v, _TN, preferred_element_type=F32)
                o_ref[r, :, vsl] = (_rms(y) * jax.nn.silu(g_ref[r, :, vsl])).astype(BF16)
            state[r, pair] = s_new


def _retention(q, k, v, gate):
    b, lp, _ = q.shape
    t = RET_TILE
    log_g = np.log(1.0 - 2.0 ** (-5.0 - np.arange(B_HEADS)))
    idx = np.arange(t, dtype=np.float64)
    rel = idx[:, None] - idx[None, :]
    dmat = np.where(rel >= 0, np.exp(np.maximum(rel, 0.0)[None] * log_g[:, None, None]), 0.0)
    qdec = np.exp((idx + 1.0)[None, :, None] * log_g[:, None, None])
    kdec = np.exp((t - 1.0 - idx)[None, :, None] * log_g[:, None, None])
    cdec = np.repeat(np.exp(t * log_g), HEAD_DIM).reshape(B_HEADS // 2, LANES, 1)
    dmat, qdec, kdec, cdec = (jnp.asarray(a, F32) for a in (dmat, qdec, kdec, cdec))

    qk_w, v_w = B_HEADS * HEAD_DIM, B_HEADS * B_VDIM
    nb = MIX_BATCH
    seq = lambda n: pl.BlockSpec((nb, t, n), lambda bi, i: (bi, i, 0))
    return pl.pallas_call(
        _retention_kernel,
        grid=(b // nb, lp // t),
        in_specs=[seq(qk_w), seq(qk_w), seq(v_w), seq(v_w),
                  _resident((B_HEADS, t, t)), _resident((B_HEADS, t, 1)),
                  _resident((B_HEADS, t, 1)), _resident((B_HEADS // 2, LANES, 1))],
        out_specs=seq(v_w),
        out_shape=jax.ShapeDtypeStruct((b, lp, v_w), BF16),
        scratch_shapes=[pltpu.VMEM((nb, B_HEADS // 2, LANES, B_VDIM), F32)],
        compiler_params=_params(("parallel", "arbitrary")),
        name="retention",
    )(q, k, v, gate, dmat, qdec, kdec, cdec)


def _swa_kernel(q_ref, k_ref, v_ref, sink_ref, o_ref):
    j = pl.program_id(1)
    nb, tq = q_ref.shape[0], q_ref.shape[1]
    lp = k_ref.shape[1]
    win = tq + C_WIN_CHUNKS * CHUNK
    rep = C_HEADS // C_KV_HEADS
    low_half = _low_half()

    start = pl.multiple_of(jnp.clip(j * tq - C_WIN_CHUNKS * CHUNK, 0, lp - win), CHUNK)
    q_row = j * tq + lax.broadcasted_iota(jnp.int32, (tq, 1), 0)
    q_col = j * tq + lax.broadcasted_iota(jnp.int32, (1, tq), 1)
    q_chunk = jnp.concatenate([q_col // CHUNK] * rep, axis=1)
    key = lax.broadcasted_iota(jnp.int32, (CHUNK + win, 1), 0)
    k_chunk = start // CHUNK + (key - CHUNK) // CHUNK
    in_window = ((key >= CHUNK) & (k_chunk >= 1) & (k_chunk <= q_chunk)
                 & (k_chunk >= q_chunk - C_WIN_CHUNKS))
    valid = in_window | ((key < CHUNK) & (key >= PAD))
    v_row_low = lax.broadcasted_iota(jnp.int32, (LANES, 1), 0) < HEAD_DIM

    for r in range(nb):
        q = jnp.where(q_row < lp, q_ref[r], jnp.zeros_like(q_ref[r]))
        for g in range(C_KV_HEADS):
            sl = slice(g * LANES, (g + 1) * LANES)
            keys = jnp.concatenate([k_ref[r, 0:CHUNK, sl], k_ref[r, pl.ds(start, win), sl]], axis=0)
            vals = jnp.concatenate([v_ref[r, 0:CHUNK, sl], v_ref[r, pl.ds(start, win), sl]], axis=0)
            heads = range(g * rep, (g + 1) * rep)
            qs = jnp.concatenate(
                [_keep_head(q[:, (h // 2) * LANES:(h // 2 + 1) * LANES], h % 2, low_half)
                 for h in heads], axis=0)
            sink = jnp.concatenate(
                [jnp.broadcast_to(sink_ref[0:1, h:h + 1] * LOG2E, (1, tq)) for h in heads], axis=1)
            s = jnp.where(valid, lax.dot_general(keys, qs, _NT, preferred_element_type=F32), NEG)
            m = jnp.maximum(jnp.max(s, axis=0, keepdims=True), sink)
            p = jnp.exp2(s - m)
            denom = jnp.sum(p, axis=0, keepdims=True) + jnp.exp2(sink - m)
            y_t = lax.dot_general(vals, p.astype(BF16), _TN, preferred_element_type=F32) / denom
            for u in range(rep // 2):
                out_t = jnp.where(v_row_low, y_t[:, 2 * u * tq:(2 * u + 1) * tq],
                                  y_t[:, (2 * u + 1) * tq:(2 * u + 2) * tq])
                pair = g * rep // 2 + u
                o_ref[r, :, pair * LANES:(pair + 1) * LANES] = jnp.where(
                    q_row >= PAD, out_t.T, 0.0).astype(BF16)


def _swa(q, k, v, sinks):
    b, lp, width = q.shape
    tq = KEY_TILE
    nb = SWA_BATCH
    kv = pl.BlockSpec((nb, lp, k.shape[2]), lambda bi, j: (bi, 0, 0))
    return pl.pallas_call(
        _swa_kernel,
        grid=(b // nb, -(-lp // tq)),
        in_specs=[pl.BlockSpec((nb, tq, width), lambda bi, j: (bi, j, 0)), kv, kv,
                  _resident((1, C_HEADS))],
        out_specs=pl.BlockSpec((nb, tq, width), lambda bi, j: (bi, j, 0)),
        out_shape=jax.ShapeDtypeStruct((b, lp, width), BF16),
        compiler_params=_params(("parallel", "parallel")),
        name="swa_sinks",
    )(q, k, v, sinks.reshape(1, C_HEADS))


def _pool_kernel(x_ref, mix_ref, scale_ref, o_ref):
    lp = x_ref.shape[1]
    row = lax.broadcasted_iota(jnp.int32, (lp, 1), 0)
    t = row - PAD
    outs = []
    for gi, w in enumerate(POOL_WINDOWS):
        x = x_ref[0, :, gi * D_GROUP:(gi + 1) * D_GROUP]
        s, shift = x, 1
        while shift < w:
            s = s + jnp.where(row >= shift, pltpu.roll(s, shift, 0), 0.0)
            shift *= 2
        count = jnp.maximum(jnp.minimum(t + 1, w), 1).astype(F32)
        y = (s / count - x).astype(BF16)
        outs.append(jnp.dot(y, mix_ref[gi].astype(BF16), preferred_element_type=F32))
    out = jnp.concatenate(outs, axis=-1) * scale_ref[...]
    o_ref[0] = jnp.where(row >= PAD, out, 0.0).astype(BF16)


def _pool(x, mix, scale):
    b, lp, width = x.shape
    n_groups = len(POOL_WINDOWS)
    return pl.pallas_call(
        _pool_kernel,
        grid=(b,),
        in_specs=[pl.BlockSpec((1, lp, width), lambda bi: (bi, 0, 0)),
                  _resident((n_groups, D_GROUP, D_GROUP)), _resident((1, width))],
        out_specs=pl.BlockSpec((1, lp, width), lambda bi: (bi, 0, 0)),
        out_shape=jax.ShapeDtypeStruct((b, lp, width), BF16),
        compiler_params=_params(("parallel",)),
        name="pool_mixer",
    )(x, mix, scale.reshape(1, width))


def kernel(x, meta_tokens, ffn1_norm, ffn1_w_in, ffn1_w_out, mix_norm, ffn2_norm, ffn2_w_in,
           ffn2_w_out, ev_w_in, ev_a_q_norm, ev_a_k_norm, ev_w_out, od_w_in, od_c_q_norm,
           od_c_k_norm, od_c_sinks, od_d_mix, od_d_scale, od_w_out):
    b, s, d = x.shape
    lp = PAD + N_META + s
    assert d == D_MODEL and (b * lp) % ROW_TILE == 0 and s % ROW_TILE == 0
    assert lp % PROJ_TILE == 0 and lp % RET_TILE == 0
    assert min(TOP_K, s // 4) == TOP_K and b % DSA_BATCH == 0 and b % MIX_BATCH == 0 and b % SWA_BATCH == 0
    depth = ffn1_norm.shape[0]
    seq = lambda a: a.reshape(b, lp, a.shape[-1])
    flat = lambda a: a.reshape(b * lp, a.shape[-1])

    ffn1_w_in, ffn1_w_out, ffn2_w_in, ffn2_w_out = (
        w.astype(BF16) for w in (ffn1_w_in, ffn1_w_out, ffn2_w_in, ffn2_w_out))
    for layer in range(depth):
        if layer == 0:
            h = _ffn_first(x, meta_tokens, ffn1_norm[0], ffn1_w_in, ffn1_w_out, 0)
        else:
            h = _ffn(h, ffn1_norm[layer], ffn1_w_in, ffn1_w_out, layer)
        if layer % 2 == 0:
            e = layer // 2
            aq, ak, av, iq, ik, iw, bq, bk, bv, bg = _even_proj(
                seq(h), mix_norm[layer], ev_w_in[e], ev_a_q_norm[e], ev_a_k_norm[e])
            ya = _dsa(aq, ak, av, iq, ik, iw)
            yb = _retention(bq, bk, bv, bg)
            mix = (flat(ya), flat(yb), ev_w_out[e].astype(BF16))
        else:
            o = layer // 2
            cq, ck, cv, dx = _odd_proj(seq(h), mix_norm[layer], od_w_in[o], od_c_q_norm[o],
                                       od_c_k_norm[o])
            yc = _swa(cq, ck, cv, od_c_sinks[o])
            yd = _pool(dx, od_d_mix[o], od_d_scale[o])
            mix = (flat(yc), flat(yd), od_w_out[o].astype(BF16))
        last = layer == depth - 1
        h = _ffn(h, ffn2_norm[layer], ffn2_w_in, ffn2_w_out, layer, (b, s) if last else None, mix)
    return h
```

```python
import functools

import jax
import jax.numpy as jnp
import numpy as np
from jax import lax
from jax.experimental import pallas as pl
from jax.experimental.pallas import tpu as pltpu

F32 = jnp.float32
BF16 = jnp.bfloat16

D_MODEL = 1024
D_FF = 2816
CHUNK = 64
N_META = 16
PAD = CHUNK - N_META
HEAD_DIM = 64
LANES = 128
A_HEADS = 8
IDX_HEADS = 8
IDX_DIM = 64
TOP_K = 256
B_HEADS = 8
B_VDIM = 128
C_HEADS = 8
C_KV_HEADS = 2
C_WIN_CHUNKS = 2
D_WIDTH = 512
POOL_WINDOWS = (2, 4, 8, 16)
D_GROUP = 128
EPS = 1e-6
NEG = -1e30
BIG = 3e38
LOG2E = 1.4426950408889634

VMEM_LIMIT = 56 * 1024 * 1024
ROW_TILE = 512
FF_CHUNK = 256
PROJ_TILE = 704
KEY_TILE = 128
KEY_UNROLL = 4
DSA_BATCH = 2
MIX_BATCH = 2
SWA_BATCH = 4
RET_TILE = 528
BISECT_ITERS = 15

_NT = (((1,), (1,)), ((), ()))
_TN = (((0,), (0,)), ((), ()))


def _params(sem):
    return pltpu.CompilerParams(dimension_semantics=sem, vmem_limit_bytes=VMEM_LIMIT)


def _rms(x, gain=None):
    y = x * lax.rsqrt(jnp.mean(x * x, axis=-1, keepdims=True) + EPS)
    return y if gain is None else y * gain


def _resident(shape):
    return pl.BlockSpec(shape, lambda *_: (0,) * len(shape), pipeline_mode=pl.Buffered(1))


def _low_half():
    return lax.broadcasted_iota(jnp.int32, (1, LANES), 1) < HEAD_DIM


def _keep_head(x, half, low_half):
    return jnp.where(low_half if half == 0 else jnp.logical_not(low_half), x, jnp.zeros_like(x))


def _ffn_kernel(h_ref, g_ref, win_ref, wout_ref, o_ref, mix_refs=None):
    x = h_ref[...]
    if mix_refs is not None:
        ya_ref, yb_ref, wmix_ref = mix_refs
        na = ya_ref.shape[1]
        x = x + jnp.dot(ya_ref[...], wmix_ref[0:na, :], preferred_element_type=F32)
        x = x + jnp.dot(yb_ref[...], wmix_ref[na:, :], preferred_element_type=F32)
    xn = _rms(x, g_ref[...]).astype(BF16)
    acc = jnp.zeros(x.shape, F32)
    for c in range(D_FF // FF_CHUNK):
        lo = c * FF_CHUNK
        g = jnp.dot(xn, win_ref[:, lo:lo + FF_CHUNK], preferred_element_type=F32)
        u = jnp.dot(xn, win_ref[:, D_FF + lo:D_FF + lo + FF_CHUNK], preferred_element_type=F32)
        a = (jax.nn.silu(g) * u).astype(BF16)
        acc = acc + jnp.dot(a, wout_ref[lo:lo + FF_CHUNK, :], preferred_element_type=F32)
    o_ref[...] = x + 0.5 * acc


def _mix_ffn_kernel(h_ref, ya_ref, yb_ref, wmix_ref, g_ref, win_ref, wout_ref, o_ref):
    _ffn_kernel(h_ref, g_ref, win_ref, wout_ref, o_ref, (ya_ref, yb_ref, wmix_ref))


def _layer_weights(shape, layer):
    return pl.BlockSpec((None,) + shape, lambda *_: (layer, 0, 0), pipeline_mode=pl.Buffered(1))


def _ffn_weight_specs(layer):
    return [_resident((1, D_MODEL)), _layer_weights((D_MODEL, 2 * D_FF), layer),
            _layer_weights((D_FF, D_MODEL), layer)]


def _frame_rows(b, lp, s, n):
    return pl.BlockSpec(
        (pl.Element(ROW_TILE), pl.Element(n)),
        lambda bi, t: (pl.multiple_of(bi * lp + lp - s + ROW_TILE * t, CHUNK), 0))


def _ffn(h, gain, w_in, w_out, layer, frames_out=None, mix=None):
    if frames_out is None:
        rows = h.shape[0]
        grid = (rows // ROW_TILE,)
        row_spec = lambda n: pl.BlockSpec((ROW_TILE, n), lambda i: (i, 0))
        out_spec = row_spec(D_MODEL)
        out_shape = jax.ShapeDtypeStruct((rows, D_MODEL), F32)
        sem = ("parallel",)
    else:
        b, s = frames_out
        grid = (b, s // ROW_TILE)
        row_spec = functools.partial(_frame_rows, b, h.shape[0] // b, s)
        out_spec = pl.BlockSpec((None, ROW_TILE, D_MODEL), lambda bi, t: (bi, t, 0))
        out_shape = jax.ShapeDtypeStruct((b, s, D_MODEL), F32)
        sem = ("parallel", "parallel")
    if mix is None:
        body, rows_in, specs = _ffn_kernel, [h], [row_spec(D_MODEL)]
    else:
        ya, yb, w_mix = mix
        body, rows_in = _mix_ffn_kernel, [h, ya, yb, w_mix]
        specs = [row_spec(D_MODEL), row_spec(ya.shape[1]), row_spec(yb.shape[1]),
                 _resident(w_mix.shape)]
    return pl.pallas_call(
        body,
        grid=grid,
        in_specs=specs + _ffn_weight_specs(layer),
        out_specs=out_spec,
        out_shape=out_shape,
        compiler_params=_params(sem),
        name="ffn",
    )(*rows_in, gain.reshape(1, D_MODEL), w_in, w_out)


def _ffn_meta_kernel(c_ref, g_ref, win_ref, wout_ref, h_hbm_ref, o_ref, y_s):
    @pl.when(pl.program_id(0) == 0)
    def _():
        _ffn_kernel(c_ref, g_ref, win_ref, wout_ref, y_s)

    o_ref[...] = y_s[...]


def _ffn_first(x, meta_tokens, gain, w_in, w_out, layer):
    b, s, _ = x.shape
    lp = PAD + N_META + s
    tiles = s // ROW_TILE
    gain = gain.reshape(1, D_MODEL)
    h = pl.pallas_call(
        _ffn_kernel,
        grid=(b, tiles),
        in_specs=[pl.BlockSpec((ROW_TILE, D_MODEL), lambda bi, t: (bi * tiles + t, 0))]
        + _ffn_weight_specs(layer),
        out_specs=_frame_rows(b, lp, s, D_MODEL),
        out_shape=jax.ShapeDtypeStruct((b * lp, D_MODEL), F32),
        compiler_params=_params(("parallel", "parallel")),
        name="ffn_frames",
    )(x.reshape(b * s, D_MODEL), gain, w_in, w_out)
    chunk0 = jnp.concatenate([jnp.zeros((PAD, D_MODEL), x.dtype), meta_tokens.astype(x.dtype)], axis=0)
    return pl.pallas_call(
        _ffn_meta_kernel,
        grid=(b,),
        in_specs=[_resident((CHUNK, D_MODEL))] + _ffn_weight_specs(layer)
        + [pl.BlockSpec(memory_space=pl.ANY)],
        out_specs=pl.BlockSpec((CHUNK, D_MODEL), lambda bi: (bi * (lp // CHUNK), 0)),
        out_shape=jax.ShapeDtypeStruct((b * lp, D_MODEL), F32),
        scratch_shapes=[pltpu.VMEM((CHUNK, D_MODEL), F32)],
        input_output_aliases={4: 0},
        compiler_params=_params(("arbitrary",)),
        name="ffn_meta",
    )(chunk0, gain, w_in, w_out, h)


def _head_norm(y, seg_ref, gain):
    sq = (y * y).astype(BF16)
    ms = jnp.concatenate(
        [jnp.dot(sq[:, p * LANES:(p + 1) * LANES], seg_ref[...], preferred_element_type=F32)
         for p in range(y.shape[1] // LANES)], axis=-1)
    return y * lax.rsqrt(ms + EPS) * gain


def _rotate_half(x, first_half):
    n = x.shape[-1]
    half = HEAD_DIM // 2
    return jnp.where(first_half, -pltpu.roll(x, n - half, 1), pltpu.roll(x, half, 1))


def _column_groups(xn, w_refs):
    state = [0, 0]

    def cols(n):
        if state[1] == w_refs[state[0]].shape[1]:
            state[:] = [state[0] + 1, 0]
        w_ref, off = w_refs[state[0]], state[1]
        state[1] += n
        return jnp.dot(xn, w_ref[:, off:off + n], preferred_element_type=F32)

    return cols


def _even_proj_kernel(h_ref, g_ref, wa_ref, wi_ref, wb_ref, seg_ref, qg_ref, kg_ref, cos_ref, sin_ref,
                      aq_ref, ak_ref, av_ref, iq_ref, ik_ref, iw_ref, bq_ref, bk_ref, bv_ref,
                      bg_ref):
    xn = _rms(h_ref[0], g_ref[...]).astype(BF16)
    cols = _column_groups(xn, (wa_ref, wi_ref, wb_ref))

    w = A_HEADS * HEAD_DIM
    aq_ref[0] = _head_norm(cols(w), seg_ref, qg_ref[...]).astype(BF16)
    ak_ref[0] = _head_norm(cols(w), seg_ref, kg_ref[...]).astype(BF16)
    av_ref[0] = cols(w).astype(BF16)
    iq_ref[0] = cols(IDX_HEADS * IDX_DIM).astype(BF16)
    ik_ref[0] = cols(LANES).astype(BF16)
    iw_ref[0] = cols(LANES) * (IDX_HEADS ** -0.5 * IDX_DIM ** -0.5)
    cos, sin = cos_ref[...], sin_ref[...]
    first_half = lax.broadcasted_iota(jnp.int32, (1, w), 1) % HEAD_DIM < HEAD_DIM // 2
    q = cols(w)
    bq_ref[0] = (q * cos + _rotate_half(q, first_half) * sin).astype(BF16)
    k = cols(w)
    bk_ref[0] = ((k * cos + _rotate_half(k, first_half) * sin) * HEAD_DIM ** -0.5).astype(BF16)
    for c in range(2):
        sl = slice(c * w, (c + 1) * w)
        bv_ref[0, :, sl] = cols(w).astype(BF16)
    for c in range(2):
        sl = slice(c * w, (c + 1) * w)
        bg_ref[0, :, sl] = cols(w)


def _odd_proj_kernel(h_ref, g_ref, w_ref, seg_ref, qg_ref, kg_ref, cq_ref, ck_ref, cv_ref, dx_ref):
    xn = _rms(h_ref[0], g_ref[...]).astype(BF16)
    cols = _column_groups(xn, (w_ref,))

    kv_w = 2 * C_KV_HEADS * HEAD_DIM
    q, k, v, dx = cols(C_HEADS * HEAD_DIM), cols(kv_w), cols(kv_w), cols(D_WIDTH)
    cq_ref[0] = _head_norm(q, seg_ref, qg_ref[...]).astype(BF16)
    ck_ref[0] = _head_norm(k, seg_ref, kg_ref[...]).astype(BF16)
    cv_ref[0] = v.astype(BF16)
    dx_ref[0] = dx


def _seg_matrix():
    g = jnp.arange(LANES) // HEAD_DIM
    return jnp.where(g[:, None] == g[None, :], 1.0 / HEAD_DIM, 0.0).astype(BF16)


def _proj_call(body, name, h, gain, weights, extra_in, extra_specs, outs):
    b, lp, _ = h.shape
    t = PROJ_TILE
    seq = lambda n: pl.BlockSpec((1, t, n), lambda bi, i: (bi, i, 0))
    return pl.pallas_call(
        body,
        grid=(b, lp // t),
        in_specs=[seq(D_MODEL), _resident((1, D_MODEL))]
        + [_resident((w.shape[0], n)) for w, n in weights]
        + [_resident((LANES, LANES))] + extra_specs,
        out_specs=[seq(n) for n, _ in outs],
        out_shape=[jax.ShapeDtypeStruct((b, lp, n), dt) for n, dt in outs],
        compiler_params=_params(("parallel", "parallel")),
        name=name,
    )(h, gain.reshape(1, D_MODEL), *[w for w, _ in weights], _seg_matrix(), *extra_in)


def _rotary_tables(lp):
    pos = np.arange(lp, dtype=np.float64) - PAD
    inv = 1.0 / (10000.0 ** (np.arange(0, HEAD_DIM, 2, dtype=np.float64) / HEAD_DIM))
    ang = pos[:, None] * inv[None]
    return (jnp.asarray(np.tile(np.cos(ang), (1, 2 * B_HEADS)), F32),
            jnp.asarray(np.tile(np.sin(ang), (1, 2 * B_HEADS)), F32))


def _even_proj(h, gain, w, q_gain, k_gain):
    lp = h.shape[1]
    w = w.astype(BF16)
    a_w = 3 * A_HEADS * HEAD_DIM + IDX_HEADS * IDX_DIM
    ik = w[:, a_w:a_w + IDX_DIM]
    iw = jnp.pad(w[:, a_w + IDX_DIM:a_w + IDX_DIM + IDX_HEADS], ((0, 0), (0, LANES - IDX_HEADS)))
    weights = [(w, a_w), (jnp.concatenate([ik, ik, iw], axis=1), 2 * LANES),
               (w[:, a_w + IDX_DIM + IDX_HEADS:], w.shape[1] - a_w - IDX_DIM - IDX_HEADS)]
    cos, sin = _rotary_tables(lp)
    w_heads = A_HEADS * HEAD_DIM
    qg = jnp.tile(q_gain, A_HEADS).reshape(1, w_heads) * (HEAD_DIM ** -0.5 * LOG2E)
    kg = jnp.tile(k_gain, A_HEADS).reshape(1, w_heads)
    table = pl.BlockSpec((PROJ_TILE, w_heads), lambda bi, i: (i, 0))
    outs = [(w_heads, BF16)] * 4 + [(LANES, BF16), (LANES, F32), (w_heads, BF16), (w_heads, BF16),
                                    (B_HEADS * B_VDIM, BF16), (B_HEADS * B_VDIM, F32)]
    return _proj_call(_even_proj_kernel, "even_in_proj", h, gain, weights,
                      [qg, kg, cos, sin],
                      [_resident((1, w_heads)), _resident((1, w_heads)), table, table], outs)


def _odd_proj(h, gain, w, q_gain, k_gain):
    w = w.astype(BF16)
    q_w = C_HEADS * HEAD_DIM
    kv_w = C_KV_HEADS * HEAD_DIM
    dup = lambda m: jnp.concatenate(
        [m[:, g * HEAD_DIM:(g + 1) * HEAD_DIM] for g in range(C_KV_HEADS) for _ in range(2)], axis=1)
    w = jnp.concatenate([w[:, :q_w], dup(w[:, q_w:q_w + kv_w]), dup(w[:, q_w + kv_w:q_w + 2 * kv_w]),
                         w[:, q_w + 2 * kv_w:]], axis=1)
    qg = jnp.tile(q_gain, C_HEADS).reshape(1, q_w) * (HEAD_DIM ** -0.5 * LOG2E)
    kg = jnp.tile(k_gain, 2 * C_KV_HEADS).reshape(1, 2 * kv_w)
    outs = [(q_w, BF16), (2 * kv_w, BF16), (2 * kv_w, BF16), (D_WIDTH, F32)]
    return _proj_call(_odd_proj_kernel, "odd_in_proj", h, gain, [(w, w.shape[1])], [qg, kg],
                      [_resident((1, q_w)), _resident((1, 2 * kv_w))], outs)


def _fold(x):
    return x.reshape(x.shape[0] // 8, 8, x.shape[1]).sum(axis=0)


def _any(mask):
    return jnp.max(jnp.where(mask, 1.0, 0.0)) > 0.0


def _topk_threshold(sc_s, n_groups, lo, hi, search):
    nq = sc_s.shape[2]

    def scan(fn, init):
        def body(g, acc):
            for u in range(KEY_UNROLL):
                c = g * KEY_UNROLL + u
                acc = fn(acc, sc_s[c], c)
            return acc
        return lax.fori_loop(0, n_groups, body, init)

    def count(pred):
        acc = scan(lambda a, s, c: a + _fold(jnp.where(pred(s, c), 1.0, 0.0)), jnp.zeros((8, nq), F32))
        return jnp.sum(acc, axis=0, keepdims=True)

    def step(lo, hi, mid):
        up = count(lambda s, c: s > mid) >= TOP_K
        return jnp.where(up, mid, lo), jnp.where(up, hi, mid)

    def coarse(_, c):
        lo, hi = c
        return step(lo, hi, 0.5 * (lo + hi))

    lo, hi = lax.fori_loop(0, BISECT_ITERS, coarse, (lo, hi))

    def bounds(lo, hi):
        def fn(acc, s, c):
            t_lo, t_hi = acc
            t_lo = jnp.minimum(t_lo, jnp.min(jnp.where(s > lo, s, BIG), axis=0, keepdims=True))
            t_hi = jnp.maximum(t_hi, jnp.max(jnp.where(s <= hi, s, -BIG), axis=0, keepdims=True))
            return t_lo, t_hi
        return scan(fn, (jnp.full((1, nq), BIG, F32), jnp.full((1, nq), -BIG, F32)))

    def unfinished(c):
        _, _, t_lo, t_hi = c
        return _any((t_lo != t_hi) & search)

    def refine(c):
        lo, hi, t_lo, t_hi = c
        mid = 0.5 * (t_lo + t_hi)
        mid = jnp.where(mid >= t_hi, t_lo, mid)
        lo, hi = step(lo, hi, mid)
        return (lo, hi) + bounds(lo, hi)

    _, _, _, thr = lax.while_loop(unfinished, refine, (lo, hi) + bounds(lo, hi))
    return thr, TOP_K - count(lambda s, c: s > thr)


def _dsa_kernel(aq_ref, iq_ref, iw_ref, ak_ref, av_ref, ik_ref, o_ref, k_s, vt_s, ik_s, sc_s, acc_s):
    j = pl.program_id(1)
    nb = aq_ref.shape[0]
    lp = ak_ref.shape[1]
    kt = KEY_TILE
    n_full, tail = divmod(lp, kt)
    pairs = A_HEADS // 2
    rows = range(nb)

    @pl.when(j == 0)
    def _():
        def put(r, c, k, ik, v):
            k_s[r, c] = k
            ik_s[r, c] = ik
            v = v.astype(F32)
            for p in range(pairs):
                vt_s[r, c, p] = v[:, p * LANES:(p + 1) * LANES].T.astype(BF16)

        def copy_tile(c, carry):
            at = pl.ds(pl.multiple_of(c * kt, kt), kt)
            for r in rows:
                put(r, c, ak_ref[r, at, :], ik_ref[r, at, :], av_ref[r, at, :])
            return carry

        lax.fori_loop(0, n_full, copy_tile, 0)
        padded = lambda a: jnp.concatenate([a, jnp.zeros((kt - tail, a.shape[1]), a.dtype)], axis=0)
        at = slice(n_full * kt, lp)
        zeros = lambda ref: jnp.zeros(ref.shape[2:], BF16)
        for r in rows:
            if tail:
                put(r, n_full, padded(ak_ref[r, at, :]), padded(ik_ref[r, at, :]),
                    padded(av_ref[r, at, :]))
            for c in range(n_full + (tail > 0), k_s.shape[1]):
                put(r, c, zeros(k_s), zeros(ik_s), zeros(k_s))
        sc_s[...] = jnp.full(sc_s.shape, NEG, F32)

    low_half = _low_half()
    q_row = j * kt + lax.broadcasted_iota(jnp.int32, (kt, 1), 0)
    q_col = j * kt + lax.broadcasted_iota(jnp.int32, (1, kt), 1)
    in_range = q_row < lp

    def both_heads(x, p):
        x = x[:, p * LANES:(p + 1) * LANES]
        return jnp.concatenate([_keep_head(x, 0, low_half), _keep_head(x, 1, low_half)], axis=0)

    aq_p, iq_p, iw_t = [], [], []
    for r in rows:
        aq = jnp.where(in_range, aq_ref[r], jnp.zeros_like(aq_ref[r]))
        iq = jnp.where(in_range, iq_ref[r], jnp.zeros_like(iq_ref[r]))
        aq_p.append([both_heads(aq, p) for p in range(pairs)])
        iq_p.append([both_heads(iq, p) for p in range(pairs)])
        iw_t.append(jnp.where(in_range, iw_ref[r], 0.0).T)
    key_grid = lax.broadcasted_iota(jnp.int32, (kt, kt), 0)
    key_limit = jnp.minimum((q_col // CHUNK + 1) * CHUNK, lp)
    n_groups = (j + KEY_UNROLL) // KEY_UNROLL
    cols = lambda r: slice(r * kt, (r + 1) * kt)

    def score_tiles(g, carry):
        mn, mx = carry
        tiles = [g * KEY_UNROLL + u for u in range(KEY_UNROLL)]
        logits = [[[lax.dot_general(ik_s[r, c], iq_p[r][p], _NT, preferred_element_type=F32)
                    for p in range(pairs)] for r in rows] for c in tiles]
        mn, mx = list(mn), list(mx)
        for c, per_row in zip(tiles, logits):
            key = key_grid + c * kt
            admissible = (key >= jnp.where(c == 0, PAD, 0)) & (key < key_limit)
            for r, lg in zip(rows, per_row):
                sc = jnp.zeros((kt, kt), F32)
                for p in range(pairs):
                    for half in range(2):
                        h = 2 * p + half
                        sc = sc + iw_t[r][h:h + 1, :] * jnp.maximum(
                            lg[p][:, half * kt:(half + 1) * kt], 0.0)
                sc_s[c, :, cols(r)] = jnp.where(admissible, sc, NEG)
                mn[r] = jnp.minimum(mn[r], jnp.min(jnp.where(admissible, sc, BIG), axis=0, keepdims=True))
                mx[r] = jnp.maximum(mx[r], jnp.max(jnp.where(admissible, sc, NEG), axis=0, keepdims=True))
        return tuple(mn), tuple(mx)

    mn, mx = lax.fori_loop(0, n_groups, score_tiles,
                           (tuple(jnp.full((1, kt), BIG, F32) for _ in rows),
                            tuple(jnp.full((1, kt), NEG, F32) for _ in rows)))
    mn = jnp.concatenate(mn, axis=1)
    mx = jnp.concatenate(mx, axis=1)

    first_search_block = (TOP_K - N_META + CHUNK - 1) // CHUNK * CHUNK // kt

    tri = (lax.broadcasted_iota(jnp.int32, (kt, kt), 0)
           >= lax.broadcasted_iota(jnp.int32, (kt, kt), 1)).astype(BF16)

    def to_bias(thr, n_equal):
        def body(g, seen):
            for u in range(KEY_UNROLL):
                c = g * KEY_UNROLL + u
                s = sc_s[c]
                equal = s == thr
                rank = seen + jnp.dot(tri, jnp.where(equal, 1.0, 0.0).astype(BF16),
                                      preferred_element_type=F32)
                sc_s[c] = jnp.where((s > thr) | (equal & (rank <= n_equal)), 0.0, NEG)
                seen = rank[kt - 1:kt, :]
            return seen
        lax.fori_loop(0, n_groups, body, jnp.zeros((1, nb * kt), F32))

    @pl.when(j < first_search_block)
    def _():
        to_bias(jnp.full((1, nb * kt), 0.5 * NEG, F32), jnp.zeros((1, nb * kt), F32))

    @pl.when(j >= first_search_block)
    def _():
        search = jnp.concatenate([q_col < lp] * nb, axis=1)
        to_bias(*_topk_threshold(sc_s, n_groups, mn - (jnp.abs(mn) * 2.0 ** -10 + 1e-30), mx, search))

    acc_s[...] = jnp.zeros(acc_s.shape, F32)
    chains = [(r, p) for r in rows for p in range(pairs)]
    ones = jnp.ones((8, KEY_UNROLL * kt), BF16)

    def attend(g, carry):
        m, l = carry
        c0 = g * KEY_UNROLL
        s = []
        for r in rows:
            bias = jnp.concatenate([sc_s[c0 + u, :, cols(r)] for u in range(KEY_UNROLL)], axis=0)
            bias = jnp.concatenate([bias, bias], axis=1)
            for p in range(pairs):
                k = jnp.concatenate([k_s[r, c0 + u, :, p * LANES:(p + 1) * LANES]
                                     for u in range(KEY_UNROLL)], axis=0)
                s.append(lax.dot_general(k, aq_p[r][p], _NT, preferred_element_type=F32) + bias)
        new_m, new_l = [], []
        for i, (r, p) in enumerate(chains):
            m_i = jnp.maximum(m[i], jnp.max(s[i], axis=0, keepdims=True))
            alpha = jnp.exp2(m[i] - m_i)
            pr = jnp.exp2((s[i] - m_i).astype(BF16))
            new_l.append(alpha * l[i] + jnp.dot(ones, pr, preferred_element_type=F32)[0:1])
            vt = jnp.concatenate([vt_s[r, c0 + u, p] for u in range(KEY_UNROLL)], axis=1)
            acc_s[r, p] = alpha * acc_s[r, p] + jnp.dot(vt, pr, preferred_element_type=F32)
            new_m.append(m_i)
        return tuple(new_m), tuple(new_l)

    init = (tuple(jnp.full((1, 2 * kt), -BIG, F32) for _ in chains),
            tuple(jnp.zeros((1, 2 * kt), F32) for _ in chains))
    _, l = lax.fori_loop(0, n_groups, attend, init)

    v_row_low = lax.broadcasted_iota(jnp.int32, (LANES, 1), 0) < HEAD_DIM
    for i, (r, p) in enumerate(chains):
        out_t = acc_s[r, p] / l[i]
        out_t = jnp.where(v_row_low, out_t[:, 0:kt], out_t[:, kt:2 * kt])
        o_ref[r, :, p * LANES:(p + 1) * LANES] = jnp.where(q_row >= PAD, out_t.T, 0.0).astype(BF16)


def _dsa(aq, ak, av, iq, ik, iw):
    b, lp, width = aq.shape
    kt = KEY_TILE
    nb = DSA_BATCH
    n_tiles = -(-lp // kt)
    n_scan = -(-n_tiles // KEY_UNROLL) * KEY_UNROLL
    q_spec = lambda n: pl.BlockSpec((nb, kt, n), lambda bi, j: (bi, j, 0))
    full_spec = lambda n: pl.BlockSpec((nb, lp, n), lambda bi, j: (bi, 0, 0))
    return pl.pallas_call(
        _dsa_kernel,
        grid=(b // nb, n_tiles),
        in_specs=[q_spec(width), q_spec(width), q_spec(LANES), full_spec(width), full_spec(width),
                  full_spec(LANES)],
        out_specs=q_spec(width),
        out_shape=jax.ShapeDtypeStruct((b, lp, width), BF16),
        scratch_shapes=[
            pltpu.VMEM((nb, n_scan, kt, width), BF16),
            pltpu.VMEM((nb, n_scan, A_HEADS // 2, LANES, kt), BF16),
            pltpu.VMEM((nb, n_scan, kt, LANES), BF16),
            pltpu.VMEM((n_scan, kt, nb * kt), F32),
            pltpu.VMEM((nb, A_HEADS // 2, LANES, 2 * kt), F32),
        ],
        compiler_params=_params(("parallel", "arbitrary")),
        name="dsa_attention",
    )(aq, iq, iw, ak, av, ik)


def _retention_kernel(q_ref, k_ref, v_ref, g_ref, dmat_ref, qdec_ref, kdec_ref, cdec_ref, o_ref,
                      state):
    i = pl.program_id(1)

    @pl.when(i == 0)
    def _():
        state[...] = jnp.zeros(state.shape, F32)

    low_half = _low_half()
    for pair in range(B_HEADS // 2):
        sl = slice(pair * LANES, (pair + 1) * LANES)
        for r in range(q_ref.shape[0]):
            q_pair = q_ref[r, :, sl]
            k_pair = k_ref[r, :, sl]
            s_old = state[r, pair]
            s_new = s_old * cdec_ref[pair]
            for half in range(2):
                h = 2 * pair + half
                vsl = slice(h * B_VDIM, (h + 1) * B_VDIM)
                v = v_ref[r, :, vsl]
                q = _keep_head(q_pair, half, low_half)
                inner = lax.dot_general(q, k_pair, _NT, preferred_element_type=F32) * dmat_ref[h]
                y = jnp.dot(inner.astype(BF16), v, preferred_element_type=F32)
                y = y + jnp.dot(q, s_old.astype(BF16), preferred_element_type=F32) * qdec_ref[h]
                kd = _keep_head((k_pair.astype(F32) * kdec_ref[h]).astype(BF16), half, low_half)
                s_new = s_new + lax.dot_general(kd, v, _TN, preferred_element_type=F32)
                o_ref[r, :, vsl] = (_rms(y) * jax.nn.silu(g_ref[r, :, vsl])).astype(BF16)
            state[r, pair] = s_new


def _retention(q, k, v, gate):
    b, lp, _ = q.shape
    t = RET_TILE
    log_g = np.log(1.0 - 2.0 ** (-5.0 - np.arange(B_HEADS)))
    idx = np.arange(t, dtype=np.float64)
    rel = idx[:, None] - idx[None, :]
    dmat = np.where(rel >= 0, np.exp(np.maximum(rel, 0.0)[None] * log_g[:, None, None]), 0.0)
    qdec = np.exp((idx + 1.0)[None, :, None] * log_g[:, None, None])
    kdec = np.exp((t - 1.0 - idx)[None, :, None] * log_g[:, None, None])
    cdec = np.repeat(np.exp(t * log_g), HEAD_DIM).reshape(B_HEADS // 2, LANES, 1)
    dmat, qdec, kdec, cdec = (jnp.asarray(a, F32) for a in (dmat, qdec, kdec, cdec))

    qk_w, v_w = B_HEADS * HEAD_DIM, B_HEADS * B_VDIM
    nb = MIX_BATCH
    seq = lambda n: pl.BlockSpec((nb, t, n), lambda bi, i: (bi, i, 0))
    return pl.pallas_call(
        _retention_kernel,
        grid=(b // nb, lp // t),
        in_specs=[seq(qk_w), seq(qk_w), seq(v_w), seq(v_w),
                  _resident((B_HEADS, t, t)), _resident((B_HEADS, t, 1)),
                  _resident((B_HEADS, t, 1)), _resident((B_HEADS // 2, LANES, 1))],
        out_specs=seq(v_w),
        out_shape=jax.ShapeDtypeStruct((b, lp, v_w), BF16),
        scratch_shapes=[pltpu.VMEM((nb, B_HEADS // 2, LANES, B_VDIM), F32)],
        compiler_params=_params(("parallel", "arbitrary")),
        name="retention",
    )(q, k, v, gate, dmat, qdec, kdec, cdec)


def _swa_kernel(q_ref, k_ref, v_ref, sink_ref, o_ref):
    j = pl.program_id(1)
    nb, tq = q_ref.shape[0], q_ref.shape[1]
    lp = k_ref.shape[1]
    win = tq + C_WIN_CHUNKS * CHUNK
    rep = C_HEADS // C_KV_HEADS
    low_half = _low_half()

    start = pl.multiple_of(jnp.clip(j * tq - C_WIN_CHUNKS * CHUNK, 0, lp - win), CHUNK)
    q_row = j * tq + lax.broadcasted_iota(jnp.int32, (tq, 1), 0)
    q_col = j * tq + lax.broadcasted_iota(jnp.int32, (1, tq), 1)
    q_chunk = jnp.concatenate([q_col // CHUNK] * rep, axis=1)
    key = lax.broadcasted_iota(jnp.int32, (CHUNK + win, 1), 0)
    k_chunk = start // CHUNK + (key - CHUNK) // CHUNK
    in_window = ((key >= CHUNK) & (k_chunk >= 1) & (k_chunk <= q_chunk)
                 & (k_chunk >= q_chunk - C_WIN_CHUNKS))
    valid = in_window | ((key < CHUNK) & (key >= PAD))
    v_row_low = lax.broadcasted_iota(jnp.int32, (LANES, 1), 0) < HEAD_DIM

    for r in range(nb):
        q = jnp.where(q_row < lp, q_ref[r], jnp.zeros_like(q_ref[r]))
        for g in range(C_KV_HEADS):
            sl = slice(g * LANES, (g + 1) * LANES)
            keys = jnp.concatenate([k_ref[r, 0:CHUNK, sl], k_ref[r, pl.ds(start, win), sl]], axis=0)
            vals = jnp.concatenate([v_ref[r, 0:CHUNK, sl], v_ref[r, pl.ds(start, win), sl]], axis=0)
            heads = range(g * rep, (g + 1) * rep)
            qs = jnp.concatenate(
                [_keep_head(q[:, (h // 2) * LANES:(h // 2 + 1) * LANES], h % 2, low_half)
                 for h in heads], axis=0)
            sink = jnp.concatenate(
                [jnp.broadcast_to(sink_ref[0:1, h:h + 1] * LOG2E, (1, tq)) for h in heads], axis=1)
            s = jnp.where(valid, lax.dot_general(keys, qs, _NT, preferred_element_type=F32), NEG)
            m = jnp.maximum(jnp.max(s, axis=0, keepdims=True), sink)
            p = jnp.exp2(s - m)
            denom = jnp.sum(p, axis=0, keepdims=True) + jnp.exp2(sink - m)
            y_t = lax.dot_general(vals, p.astype(BF16), _TN, preferred_element_type=F32) / denom
            for u in range(rep // 2):
                out_t = jnp.where(v_row_low, y_t[:, 2 * u * tq:(2 * u + 1) * tq],
                                  y_t[:, (2 * u + 1) * tq:(2 * u + 2) * tq])
                pair = g * rep // 2 + u
                o_ref[r, :, pair * LANES:(pair + 1) * LANES] = jnp.where(
                    q_row >= PAD, out_t.T, 0.0).astype(BF16)


def _swa(q, k, v, sinks):
    b, lp, width = q.shape
    tq = KEY_TILE
    nb = SWA_BATCH
    kv = pl.BlockSpec((nb, lp, k.shape[2]), lambda bi, j: (bi, 0, 0))
    return pl.pallas_call(
        _swa_kernel,
        grid=(b // nb, -(-lp // tq)),
        in_specs=[pl.BlockSpec((nb, tq, width), lambda bi, j: (bi, j, 0)), kv, kv,
                  _resident((1, C_HEADS))],
        out_specs=pl.BlockSpec((nb, tq, width), lambda bi, j: (bi, j, 0)),
        out_shape=jax.ShapeDtypeStruct((b, lp, width), BF16),
        compiler_params=_params(("parallel", "parallel")),
        name="swa_sinks",
    )(q, k, v, sinks.reshape(1, C_HEADS))


def _pool_kernel(x_ref, mix_ref, scale_ref, o_ref):
    lp = x_ref.shape[1]
    row = lax.broadcasted_iota(jnp.int32, (lp, 1), 0)
    t = row - PAD
    outs = []
    for gi, w in enumerate(POOL_WINDOWS):
        x = x_ref[0, :, gi * D_GROUP:(gi + 1) * D_GROUP]
        s, shift = x, 1
        while shift < w:
            s = s + jnp.where(row >= shift, pltpu.roll(s, shift, 0), 0.0)
            shift *= 2
        count = jnp.maximum(jnp.minimum(t + 1, w), 1).astype(F32)
        y = (s / count - x).astype(BF16)
        outs.append(jnp.dot(y, mix_ref[gi].astype(BF16), preferred_element_type=F32))
    out = jnp.concatenate(outs, axis=-1) * scale_ref[...]
    o_ref[0] = jnp.where(row >= PAD, out, 0.0).astype(BF16)


def _pool(x, mix, scale):
    b, lp, width = x.shape
    n_groups = len(POOL_WINDOWS)
    return pl.pallas_call(
        _pool_kernel,
        grid=(b,),
        in_specs=[pl.BlockSpec((1, lp, width), lambda bi: (bi, 0, 0)),
                  _resident((n_groups, D_GROUP, D_GROUP)), _resident((1, width))],
        out_specs=pl.BlockSpec((1, lp, width), lambda bi: (bi, 0, 0)),
        out_shape=jax.ShapeDtypeStruct((b, lp, width), BF16),
        compiler_params=_params(("parallel",)),
        name="pool_mixer",
    )(x, mix, scale.reshape(1, width))


def kernel(x, meta_tokens, ffn1_norm, ffn1_w_in, ffn1_w_out, mix_norm, ffn2_norm, ffn2_w_in,
           ffn2_w_out, ev_w_in, ev_a_q_norm, ev_a_k_norm, ev_w_out, od_w_in, od_c_q_norm,
           od_c_k_norm, od_c_sinks, od_d_mix, od_d_scale, od_w_out):
    b, s, d = x.shape
    lp = PAD + N_META + s
    assert d == D_MODEL and (b * lp) % ROW_TILE == 0 and s % ROW_TILE == 0
    assert lp % PROJ_TILE == 0 and lp % RET_TILE == 0
    assert min(TOP_K, s // 4) == TOP_K and b % DSA_BATCH == 0 and b % MIX_BATCH == 0 and b % SWA_BATCH == 0
    depth = ffn1_norm.shape[0]
    seq = lambda a: a.reshape(b, lp, a.shape[-1])
    flat = lambda a: a.reshape(b * lp, a.shape[-1])

    ffn1_w_in, ffn1_w_out, ffn2_w_in, ffn2_w_out = (
        w.astype(BF16) for w in (ffn1_w_in, ffn1_w_out, ffn2_w_in, ffn2_w_out))
    for layer in range(depth):
        if layer == 0:
            h = _ffn_first(x, meta_tokens, ffn1_norm[0], ffn1_w_in, ffn1_w_out, 0)
        else:
            h = _ffn(h, ffn1_norm[layer], ffn1_w_in, ffn1_w_out, layer)
        if layer % 2 == 0:
            e = layer // 2
            aq, ak, av, iq, ik, iw, bq, bk, bv, bg = _even_proj(
                seq(h), mix_norm[layer], ev_w_in[e], ev_a_q_norm[e], ev_a_k_norm[e])
            ya = _dsa(aq, ak, av, iq, ik, iw)
            yb = _retention(bq, bk, bv, bg)
            mix = (flat(ya), flat(yb), ev_w_out[e].astype(BF16))
        else:
            o = layer // 2
            cq, ck, cv, dx = _odd_proj(seq(h), mix_norm[layer], od_w_in[o], od_c_q_norm[o],
                                       od_c_k_norm[o])
            yc = _swa(cq, ck, cv, od_c_sinks[o])
            yd = _pool(dx, od_d_mix[o], od_d_scale[o])
            mix = (flat(yc), flat(yd), od_w_out[o].astype(BF16))
        last = layer == depth - 1
        h = _ffn(h, ffn2_norm[layer], ffn2_w_in, ffn2_w_out, layer, (b, s) if last else None, mix)
    return h
```

```python
import functools

import jax
import jax.numpy as jnp
import numpy as np
from jax import lax
from jax.experimental import pallas as pl
from jax.experimental.pallas import tpu as pltpu

F32 = jnp.float32
BF16 = jnp.bfloat16

D_MODEL = 1024
D_FF = 2816
CHUNK = 64
N_META = 16
PAD = CHUNK - N_META
HEAD_DIM = 64
LANES = 128
A_HEADS = 8
IDX_HEADS = 8
IDX_DIM = 64
TOP_K = 256
B_HEADS = 8
B_VDIM = 128
C_HEADS = 8
C_KV_HEADS = 2
C_WIN_CHUNKS = 2
D_WIDTH = 512
POOL_WINDOWS = (2, 4, 8, 16)
D_GROUP = 128
EPS = 1e-6
NEG = -1e30
BIG = 3e38
LOG2E = 1.4426950408889634

VMEM_LIMIT = 56 * 1024 * 1024
ROW_TILE = 512
FF_CHUNK = 256
PROJ_TILE = 704
KEY_TILE = 128
KEY_UNROLL = 4
DSA_BATCH = 2
MIX_BATCH = 2
SWA_BATCH = 4
RET_TILE = 528
BISECT_ITERS = 15

_NT = (((1,), (1,)), ((), ()))
_TN = (((0,), (0,)), ((), ()))


def _params(sem):
    return pltpu.CompilerParams(dimension_semantics=sem, vmem_limit_bytes=VMEM_LIMIT)


def _rms(x, gain=None):
    y = x * lax.rsqrt(jnp.mean(x * x, axis=-1, keepdims=True) + EPS)
    return y if gain is None else y * gain


def _resident(shape):
    return pl.BlockSpec(shape, lambda *_: (0,) * len(shape), pipeline_mode=pl.Buffered(1))


def _low_half():
    return lax.broadcasted_iota(jnp.int32, (1, LANES), 1) < HEAD_DIM


def _keep_head(x, half, low_half):
    return jnp.where(low_half if half == 0 else jnp.logical_not(low_half), x, jnp.zeros_like(x))


def _ffn_kernel(h_ref, g_ref, win_ref, wout_ref, o_ref, mix_refs=None):
    x = h_ref[...]
    if mix_refs is not None:
        ya_ref, yb_ref, wmix_ref = mix_refs
        na = ya_ref.shape[1]
        x = x + jnp.dot(ya_ref[...], wmix_ref[0:na, :], preferred_element_type=F32)
        x = x + jnp.dot(yb_ref[...], wmix_ref[na:, :], preferred_element_type=F32)
    xn = _rms(x, g_ref[...]).astype(BF16)
    acc = jnp.zeros(x.shape, F32)
    for c in range(D_FF // FF_CHUNK):
        lo = c * FF_CHUNK
        g = jnp.dot(xn, win_ref[:, lo:lo + FF_CHUNK], preferred_element_type=F32)
        u = jnp.dot(xn, win_ref[:, D_FF + lo:D_FF + lo + FF_CHUNK], preferred_element_type=F32)
        a = (jax.nn.silu(g) * u).astype(BF16)
        acc = acc + jnp.dot(a, wout_ref[lo:lo + FF_CHUNK, :], preferred_element_type=F32)
    o_ref[...] = x + 0.5 * acc


def _mix_ffn_kernel(h_ref, ya_ref, yb_ref, wmix_ref, g_ref, win_ref, wout_ref, o_ref):
    _ffn_kernel(h_ref, g_ref, win_ref, wout_ref, o_ref, (ya_ref, yb_ref, wmix_ref))


def _layer_weights(shape, layer):
    return pl.BlockSpec((None,) + shape, lambda *_: (layer, 0, 0), pipeline_mode=pl.Buffered(1))


def _ffn_weight_specs(layer):
    return [_resident((1, D_MODEL)), _layer_weights((D_MODEL, 2 * D_FF), layer),
            _layer_weights((D_FF, D_MODEL), layer)]


def _frame_rows(b, lp, s, n):
    return pl.BlockSpec(
        (pl.Element(ROW_TILE), pl.Element(n)),
        lambda bi, t: (pl.multiple_of(bi * lp + lp - s + ROW_TILE * t, CHUNK), 0))


def _ffn(h, gain, w_in, w_out, layer, frames_out=None, mix=None):
    if frames_out is None:
        rows = h.shape[0]
        grid = (rows // ROW_TILE,)
        row_spec = lambda n: pl.BlockSpec((ROW_TILE, n), lambda i: (i, 0))
        out_spec = row_spec(D_MODEL)
        out_shape = jax.ShapeDtypeStruct((rows, D_MODEL), F32)
        sem = ("parallel",)
    else:
        b, s = frames_out
        grid = (b, s // ROW_TILE)
        row_spec = functools.partial(_frame_rows, b, h.shape[0] // b, s)
        out_spec = pl.BlockSpec((None, ROW_TILE, D_MODEL), lambda bi, t: (bi, t, 0))
        out_shape = jax.ShapeDtypeStruct((b, s, D_MODEL), F32)
        sem = ("parallel", "parallel")
    if mix is None:
        body, rows_in, specs = _ffn_kernel, [h], [row_spec(D_MODEL)]
    else:
        ya, yb, w_mix = mix
        body, rows_in = _mix_ffn_kernel, [h, ya, yb, w_mix]
        specs = [row_spec(D_MODEL), row_spec(ya.shape[1]), row_spec(yb.shape[1]),
                 _resident(w_mix.shape)]
    return pl.pallas_call(
        body,
        grid=grid,
        in_specs=specs + _ffn_weight_specs(layer),
        out_specs=out_spec,
        out_shape=out_shape,
        compiler_params=_params(sem),
        name="ffn",
    )(*rows_in, gain.reshape(1, D_MODEL), w_in, w_out)


def _ffn_meta_kernel(c_ref, g_ref, win_ref, wout_ref, h_hbm_ref, o_ref, y_s):
    @pl.when(pl.program_id(0) == 0)
    def _():
        _ffn_kernel(c_ref, g_ref, win_ref, wout_ref, y_s)

    o_ref[...] = y_s[...]


def _ffn_first(x, meta_tokens, gain, w_in, w_out, layer):
    b, s, _ = x.shape
    lp = PAD + N_META + s
    tiles = s // ROW_TILE
    gain = gain.reshape(1, D_MODEL)
    h = pl.pallas_call(
        _ffn_kernel,
        grid=(b, tiles),
        in_specs=[pl.BlockSpec((ROW_TILE, D_MODEL), lambda bi, t: (bi * tiles + t, 0))]
        + _ffn_weight_specs(layer),
        out_specs=_frame_rows(b, lp, s, D_MODEL),
        out_shape=jax.ShapeDtypeStruct((b * lp, D_MODEL), F32),
        compiler_params=_params(("parallel", "parallel")),
        name="ffn_frames",
    )(x.reshape(b * s, D_MODEL), gain, w_in, w_out)
    chunk0 = jnp.concatenate([jnp.zeros((PAD, D_MODEL), x.dtype), meta_tokens.astype(x.dtype)], axis=0)
    return pl.pallas_call(
        _ffn_meta_kernel,
        grid=(b,),
        in_specs=[_resident((CHUNK, D_MODEL))] + _ffn_weight_specs(layer)
        + [pl.BlockSpec(memory_space=pl.ANY)],
        out_specs=pl.BlockSpec((CHUNK, D_MODEL), lambda bi: (bi * (lp // CHUNK), 0)),
        out_shape=jax.ShapeDtypeStruct((b * lp, D_MODEL), F32),
        scratch_shapes=[pltpu.VMEM((CHUNK, D_MODEL), F32)],
        input_output_aliases={4: 0},
        compiler_params=_params(("arbitrary",)),
        name="ffn_meta",
    )(chunk0, gain, w_in, w_out, h)


def _head_norm(y, seg_ref, gain):
    sq = (y * y).astype(BF16)
    ms = jnp.concatenate(
        [jnp.dot(sq[:, p * LANES:(p + 1) * LANES], seg_ref[...], preferred_element_type=F32)
         for p in range(y.shape[1] // LANES)], axis=-1)
    return y * lax.rsqrt(ms + EPS) * gain


def _rotate_half(x, first_half):
    n = x.shape[-1]
    half = HEAD_DIM // 2
    return jnp.where(first_half, -pltpu.roll(x, n - half, 1), pltpu.roll(x, half, 1))


def _column_groups(xn, w_refs):
    state = [0, 0]

    def cols(n):
        if state[1] == w_refs[state[0]].shape[1]:
            state[:] = [state[0] + 1, 0]
        w_ref, off = w_refs[state[0]], state[1]
        state[1] += n
        return jnp.dot(xn, w_ref[:, off:off + n], preferred_element_type=F32)

    return cols


def _even_proj_kernel(h_ref, g_ref, wa_ref, wi_ref, wb_ref, seg_ref, qg_ref, kg_ref, cos_ref, sin_ref,
                      aq_ref, ak_ref, av_ref, iq_ref, ik_ref, iw_ref, bq_ref, bk_ref, bv_ref,
                      bg_ref):
    xn = _rms(h_ref[0], g_ref[...]).astype(BF16)
    cols = _column_groups(xn, (wa_ref, wi_ref, wb_ref))

    w = A_HEADS * HEAD_DIM
    aq_ref[0] = _head_norm(cols(w), seg_ref, qg_ref[...]).astype(BF16)
    ak_ref[0] = _head_norm(cols(w), seg_ref, kg_ref[...]).astype(BF16)
    av_ref[0] = cols(w).astype(BF16)
    iq_ref[0] = cols(IDX_HEADS * IDX_DIM).astype(BF16)
    ik_ref[0] = cols(LANES).astype(BF16)
    iw_ref[0] = cols(LANES) * (IDX_HEADS ** -0.5 * IDX_DIM ** -0.5)
    cos, sin = cos_ref[...], sin_ref[...]
    first_half = lax.broadcasted_iota(jnp.int32, (1, w), 1) % HEAD_DIM < HEAD_DIM // 2
    q = cols(w)
    bq_ref[0] = (q * cos + _rotate_half(q, first_half) * sin).astype(BF16)
    k = cols(w)
    bk_ref[0] = ((k * cos + _rotate_half(k, first_half) * sin) * HEAD_DIM ** -0.5).astype(BF16)
    for c in range(2):
        sl = slice(c * w, (c + 1) * w)
        bv_ref[0, :, sl] = cols(w).astype(BF16)
    for c in range(2):
        sl = slice(c * w, (c + 1) * w)
        bg_ref[0, :, sl] = cols(w)


def _odd_proj_kernel(h_ref, g_ref, w_ref, seg_ref, qg_ref, kg_ref, cq_ref, ck_ref, cv_ref, dx_ref):
    xn = _rms(h_ref[0], g_ref[...]).astype(BF16)
    cols = _column_groups(xn, (w_ref,))

    kv_w = 2 * C_KV_HEADS * HEAD_DIM
    q, k, v, dx = cols(C_HEADS * HEAD_DIM), cols(kv_w), cols(kv_w), cols(D_WIDTH)
    cq_ref[0] = _head_norm(q, seg_ref, qg_ref[...]).astype(BF16)
    ck_ref[0] = _head_norm(k, seg_ref, kg_ref[...]).astype(BF16)
    cv_ref[0] = v.astype(BF16)
    dx_ref[0] = dx


def _seg_matrix():
    g = jnp.arange(LANES) // HEAD_DIM
    return jnp.where(g[:, None] == g[None, :], 1.0 / HEAD_DIM, 0.0).astype(BF16)


def _proj_call(body, name, h, gain, weights, extra_in, extra_specs, outs):
    b, lp, _ = h.shape
    t = PROJ_TILE
    seq = lambda n: pl.BlockSpec((1, t, n), lambda bi, i: (bi, i, 0))
    return pl.pallas_call(
        body,
        grid=(b, lp // t),
        in_specs=[seq(D_MODEL), _resident((1, D_MODEL))]
        + [_resident((w.shape[0], n)) for w, n in weights]
        + [_resident((LANES, LANES))] + extra_specs,
        out_specs=[seq(n) for n, _ in outs],
        out_shape=[jax.ShapeDtypeStruct((b, lp, n), dt) for n, dt in outs],
        compiler_params=_params(("parallel", "parallel")),
        name=name,
    )(h, gain.reshape(1, D_MODEL), *[w for w, _ in weights], _seg_matrix(), *extra_in)


def _rotary_tables(lp):
    pos = np.arange(lp, dtype=np.float64) - PAD
    inv = 1.0 / (10000.0 ** (np.arange(0, HEAD_DIM, 2, dtype=np.float64) / HEAD_DIM))
    ang = pos[:, None] * inv[None]
    return (jnp.asarray(np.tile(np.cos(ang), (1, 2 * B_HEADS)), F32),
            jnp.asarray(np.tile(np.sin(ang), (1, 2 * B_HEADS)), F32))


def _even_proj(h, gain, w, q_gain, k_gain):
    lp = h.shape[1]
    w = w.astype(BF16)
    a_w = 3 * A_HEADS * HEAD_DIM + IDX_HEADS * IDX_DIM
    ik = w[:, a_w:a_w + IDX_DIM]
    iw = jnp.pad(w[:, a_w + IDX_DIM:a_w + IDX_DIM + IDX_HEADS], ((0, 0), (0, LANES - IDX_HEADS)))
    weights = [(w, a_w), (jnp.concatenate([ik, ik, iw], axis=1), 2 * LANES),
               (w[:, a_w + IDX_DIM + IDX_HEADS:], w.shape[1] - a_w - IDX_DIM - IDX_HEADS)]
    cos, sin = _rotary_tables(lp)
    w_heads = A_HEADS * HEAD_DIM
    qg = jnp.tile(q_gain, A_HEADS).reshape(1, w_heads) * (HEAD_DIM ** -0.5 * LOG2E)
    kg = jnp.tile(k_gain, A_HEADS).reshape(1, w_heads)
    table = pl.BlockSpec((PROJ_TILE, w_heads), lambda bi, i: (i, 0))
    outs = [(w_heads, BF16)] * 4 + [(LANES, BF16), (LANES, F32), (w_heads, BF16), (w_heads, BF16),
                                    (B_HEADS * B_VDIM, BF16), (B_HEADS * B_VDIM, F32)]
    return _proj_call(_even_proj_kernel, "even_in_proj", h, gain, weights,
                      [qg, kg, cos, sin],
                      [_resident((1, w_heads)), _resident((1, w_heads)), table, table], outs)


def _odd_proj(h, gain, w, q_gain, k_gain):
    w = w.astype(BF16)
    q_w = C_HEADS * HEAD_DIM
    kv_w = C_KV_HEADS * HEAD_DIM
    dup = lambda m: jnp.concatenate(
        [m[:, g * HEAD_DIM:(g + 1) * HEAD_DIM] for g in range(C_KV_HEADS) for _ in range(2)], axis=1)
    w = jnp.concatenate([w[:, :q_w], dup(w[:, q_w:q_w + kv_w]), dup(w[:, q_w + kv_w:q_w + 2 * kv_w]),
                         w[:, q_w + 2 * kv_w:]], axis=1)
    qg = jnp.tile(q_gain, C_HEADS).reshape(1, q_w) * (HEAD_DIM ** -0.5 * LOG2E)
    kg = jnp.tile(k_gain, 2 * C_KV_HEADS).reshape(1, 2 * kv_w)
    outs = [(q_w, BF16), (2 * kv_w, BF16), (2 * kv_w, BF16), (D_WIDTH, F32)]
    return _proj_call(_odd_proj_kernel, "odd_in_proj", h, gain, [(w, w.shape[1])], [qg, kg],
                      [_resident((1, q_w)), _resident((1, 2 * kv_w))], outs)


def _fold(x):
    return x.reshape(x.shape[0] // 8, 8, x.shape[1]).sum(axis=0)


def _any(mask):
    return jnp.max(jnp.where(mask, 1.0, 0.0)) > 0.0


def _topk_threshold(sc_s, n_groups, lo, hi, search):
    nq = sc_s.shape[2]

    def scan(fn, init):
        def body(g, acc):
            for u in range(KEY_UNROLL):
                c = g * KEY_UNROLL + u
                acc = fn(acc, sc_s[c], c)
            return acc
        return lax.fori_loop(0, n_groups, body, init)

    def count(pred):
        acc = scan(lambda a, s, c: a + _fold(jnp.where(pred(s, c), 1.0, 0.0)), jnp.zeros((8, nq), F32))
        return jnp.sum(acc, axis=0, keepdims=True)

    def step(lo, hi, mid):
        up = count(lambda s, c: s > mid) >= TOP_K
        return jnp.where(up, mid, lo), jnp.where(up, hi, mid)

    def coarse(_, c):
        lo, hi = c
        return step(lo, hi, 0.5 * (lo + hi))

    lo, hi = lax.fori_loop(0, BISECT_ITERS, coarse, (lo, hi))

    def bounds(lo, hi):
        def fn(acc, s, c):
            t_lo, t_hi = acc
            t_lo = jnp.minimum(t_lo, jnp.min(jnp.where(s > lo, s, BIG), axis=0, keepdims=True))
            t_hi = jnp.maximum(t_hi, jnp.max(jnp.where(s <= hi, s, -BIG), axis=0, keepdims=True))
            return t_lo, t_hi
        return scan(fn, (jnp.full((1, nq), BIG, F32), jnp.full((1, nq), -BIG, F32)))

    def unfinished(c):
        _, _, t_lo, t_hi = c
        return _any((t_lo != t_hi) & search)

    def refine(c):
        lo, hi, t_lo, t_hi = c
        mid = 0.5 * (t_lo + t_hi)
        mid = jnp.where(mid >= t_hi, t_lo, mid)
        lo, hi = step(lo, hi, mid)
        return (lo, hi) + bounds(lo, hi)

    _, _, _, thr = lax.while_loop(unfinished, refine, (lo, hi) + bounds(lo, hi))
    return thr, TOP_K - count(lambda s, c: s > thr)


def _dsa_kernel(aq_ref, iq_ref, iw_ref, ak_ref, av_ref, ik_ref, o_ref, k_s, vt_s, ik_s, sc_s, acc_s):
    j = pl.program_id(1)
    nb = aq_ref.shape[0]
    lp = ak_ref.shape[1]
    kt = KEY_TILE
    n_full, tail = divmod(lp, kt)
    pairs = A_HEADS // 2
    rows = range(nb)

    @pl.when(j == 0)
    def _():
        def put(r, c, k, ik, v):
            k_s[r, c] = k
            ik_s[r, c] = ik
            v = v.astype(F32)
            for p in range(pairs):
                vt_s[r, c, p] = v[:, p * LANES:(p + 1) * LANES].T.astype(BF16)

        def copy_tile(c, carry):
            at = pl.ds(pl.multiple_of(c * kt, kt), kt)
            for r in rows:
                put(r, c, ak_ref[r, at, :], ik_ref[r, at, :], av_ref[r, at, :])
            return carry

        lax.fori_loop(0, n_full, copy_tile, 0)
        padded = lambda a: jnp.concatenate([a, jnp.zeros((kt - tail, a.shape[1]), a.dtype)], axis=0)
        at = slice(n_full * kt, lp)
        zeros = lambda ref: jnp.zeros(ref.shape[2:], BF16)
        for r in rows:
            if tail:
                put(r, n_full, padded(ak_ref[r, at, :]), padded(ik_ref[r, at, :]),
                    padded(av_ref[r, at, :]))
            for c in range(n_full + (tail > 0), k_s.shape[1]):
                put(r, c, zeros(k_s), zeros(ik_s), zeros(k_s))
        sc_s[...] = jnp.full(sc_s.shape, NEG, F32)

    low_half = _low_half()
    q_row = j * kt + lax.broadcasted_iota(jnp.int32, (kt, 1), 0)
    q_col = j * kt + lax.broadcasted_iota(jnp.int32, (1, kt), 1)
    in_range = q_row < lp

    def both_heads(x, p):
        x = x[:, p * LANES:(p + 1) * LANES]
        return jnp.concatenate([_keep_head(x, 0, low_half), _keep_head(x, 1, low_half)], axis=0)

    aq_p, iq_p, iw_t = [], [], []
    for r in rows:
        aq = jnp.where(in_range, aq_ref[r], jnp.zeros_like(aq_ref[r]))
        iq = jnp.where(in_range, iq_ref[r], jnp.zeros_like(iq_ref[r]))
        aq_p.append([both_heads(aq, p) for p in range(pairs)])
        iq_p.append([both_heads(iq, p) for p in range(pairs)])
        iw_t.append(jnp.where(in_range, iw_ref[r], 0.0).T)
    key_grid = lax.broadcasted_iota(jnp.int32, (kt, kt), 0)
    key_limit = jnp.minimum((q_col // CHUNK + 1) * CHUNK, lp)
    n_groups = (j + KEY_UNROLL) // KEY_UNROLL
    cols = lambda r: slice(r * kt, (r + 1) * kt)

    def score_tiles(g, carry):
        mn, mx = carry
        tiles = [g * KEY_UNROLL + u for u in range(KEY_UNROLL)]
        logits = [[[lax.dot_general(ik_s[r, c], iq_p[r][p], _NT, preferred_element_type=F32)
                    for p in range(pairs)] for r in rows] for c in tiles]
        mn, mx = list(mn), list(mx)
        for c, per_row in zip(tiles, logits):
            key = key_grid + c * kt
            admissible = (key >= jnp.where(c == 0, PAD, 0)) & (key < key_limit)
            for r, lg in zip(rows, per_row):
                sc = jnp.zeros((kt, kt), F32)
                for p in range(pairs):
                    for half in range(2):
                        h = 2 * p + half
                        sc = sc + iw_t[r][h:h + 1, :] * jnp.maximum(
                            lg[p][:, half * kt:(half + 1) * kt], 0.0)
                sc_s[c, :, cols(r)] = jnp.where(admissible, sc, NEG)
                mn[r] = jnp.minimum(mn[r], jnp.min(jnp.where(admissible, sc, BIG), axis=0, keepdims=True))
                mx[r] = jnp.maximum(mx[r], jnp.max(jnp.where(admissible, sc, NEG), axis=0, keepdims=True))
        return tuple(mn), tuple(mx)

    mn, mx = lax.fori_loop(0, n_groups, score_tiles,
                           (tuple(jnp.full((1, kt), BIG, F32) for _ in rows),
                            tuple(jnp.full((1, kt), NEG, F32) for _ in rows)))
    mn = jnp.concatenate(mn, axis=1)
    mx = jnp.concatenate(mx, axis=1)

    first_search_block = (TOP_K - N_META + CHUNK - 1) // CHUNK * CHUNK // kt

    tri = (lax.broadcasted_iota(jnp.int32, (kt, kt), 0)
           >= lax.broadcasted_iota(jnp.int32, (kt, kt), 1)).astype(BF16)

    def to_bias(thr, n_equal):
        def body(g, seen):
            for u in range(KEY_UNROLL):
                c = g * KEY_UNROLL + u
                s = sc_s[c]
                equal = s == thr
                rank = seen + jnp.dot(tri, jnp.where(equal, 1.0, 0.0).astype(BF16),
                                      preferred_element_type=F32)
                sc_s[c] = jnp.where((s > thr) | (equal & (rank <= n_equal)), 0.0, NEG)
                seen = rank[kt - 1:kt, :]
            return seen
        lax.fori_loop(0, n_groups, body, jnp.zeros((1, nb * kt), F32))

    @pl.when(j < first_search_block)
    def _():
        to_bias(jnp.full((1, nb * kt), 0.5 * NEG, F32), jnp.zeros((1, nb * kt), F32))

    @pl.when(j >= first_search_block)
    def _():
        search = jnp.concatenate([q_col < lp] * nb, axis=1)
        to_bias(*_topk_threshold(sc_s, n_groups, mn - (jnp.abs(mn) * 2.0 ** -10 + 1e-30), mx, search))

    acc_s[...] = jnp.zeros(acc_s.shape, F32)
    chains = [(r, p) for r in rows for p in range(pairs)]
    ones = jnp.ones((8, KEY_UNROLL * kt), BF16)

    def attend(g, carry):
        m, l = carry
        c0 = g * KEY_UNROLL
        s = []
        for r in rows:
            bias = jnp.concatenate([sc_s[c0 + u, :, cols(r)] for u in range(KEY_UNROLL)], axis=0)
            bias = jnp.concatenate([bias, bias], axis=1)
            for p in range(pairs):
                k = jnp.concatenate([k_s[r, c0 + u, :, p * LANES:(p + 1) * LANES]
                                     for u in range(KEY_UNROLL)], axis=0)
                s.append(lax.dot_general(k, aq_p[r][p], _NT, preferred_element_type=F32) + bias)
        new_m, new_l = [], []
        for i, (r, p) in enumerate(chains):
            m_i = jnp.maximum(m[i], jnp.max(s[i], axis=0, keepdims=True))
            alpha = jnp.exp2(m[i] - m_i)
            pr = jnp.exp2((s[i] - m_i).astype(BF16))
            new_l.append(alpha * l[i] + jnp.dot(ones, pr, preferred_element_type=F32)[0:1])
            vt = jnp.concatenate([vt_s[r, c0 + u, p] for u in range(KEY_UNROLL)], axis=1)
            acc_s[r, p] = alpha * acc_s[r, p] + jnp.dot(vt, pr, preferred_element_type=F32)
            new_m.append(m_i)
        return tuple(new_m), tuple(new_l)

    init = (tuple(jnp.full((1, 2 * kt), -BIG, F32) for _ in chains),
            tuple(jnp.zeros((1, 2 * kt), F32) for _ in chains))
    _, l = lax.fori_loop(0, n_groups, attend, init)

    v_row_low = lax.broadcasted_iota(jnp.int32, (LANES, 1), 0) < HEAD_DIM
    for i, (r, p) in enumerate(chains):
        out_t = acc_s[r, p] / l[i]
        out_t = jnp.where(v_row_low, out_t[:, 0:kt], out_t[:, kt:2 * kt])
        o_ref[r, :, p * LANES:(p + 1) * LANES] = jnp.where(q_row >= PAD, out_t.T, 0.0).astype(BF16)


def _dsa(aq, ak, av, iq, ik, iw):
    b, lp, width = aq.shape
    kt = KEY_TILE
    nb = DSA_BATCH
    n_tiles = -(-lp // kt)
    n_scan = -(-n_tiles // KEY_UNROLL) * KEY_UNROLL
    q_spec = lambda n: pl.BlockSpec((nb, kt, n), lambda bi, j: (bi, j, 0))
    full_spec = lambda n: pl.BlockSpec((nb, lp, n), lambda bi, j: (bi, 0, 0))
    return pl.pallas_call(
        _dsa_kernel,
        grid=(b // nb, n_tiles),
        in_specs=[q_spec(width), q_spec(width), q_spec(LANES), full_spec(width), full_spec(width),
                  full_spec(LANES)],
        out_specs=q_spec(width),
        out_shape=jax.ShapeDtypeStruct((b, lp, width), BF16),
        scratch_shapes=[
            pltpu.VMEM((nb, n_scan, kt, width), BF16),
            pltpu.VMEM((nb, n_scan, A_HEADS // 2, LANES, kt), BF16),
            pltpu.VMEM((nb, n_scan, kt, LANES), BF16),
            pltpu.VMEM((n_scan, kt, nb * kt), F32),
            pltpu.VMEM((nb, A_HEADS // 2, LANES, 2 * kt), F32),
        ],
        compiler_params=_params(("parallel", "arbitrary")),
        name="dsa_attention",
    )(aq, iq, iw, ak, av, ik)


def _retention_kernel(q_ref, k_ref, v_ref, g_ref, dmat_ref, qdec_ref, kdec_ref, cdec_ref, o_ref,
                      state):
    i = pl.program_id(1)

    @pl.when(i == 0)
    def _():
        state[...] = jnp.zeros(state.shape, F32)

    low_half = _low_half()
    for pair in range(B_HEADS // 2):
        sl = slice(pair * LANES, (pair + 1) * LANES)
        for r in range(q_ref.shape[0]):
            q_pair = q_ref[r, :, sl]
            k_pair = k_ref[r, :, sl]
            s_old = state[r, pair]
            s_new = s_old * cdec_ref[pair]
            for half in range(2):
                h = 2 * pair + half
                vsl = slice(h * B_VDIM, (h + 1) * B_VDIM)
                v = v_ref[r, :, vsl]
                q = _keep_head(q_pair, half, low_half)
                inner = lax.dot_general(q, k_pair, _NT, preferred_element_type=F32) * dmat_ref[h]
                y = jnp.dot(inner.astype(BF16), v, preferred_element_type=F32)
                y = y + jnp.dot(q, s_old.astype(BF16), preferred_element_type=F32) * qdec_ref[h]
                kd = _keep_head((k_pair.astype(F32) * kdec_ref[h]).astype(BF16), half, low_half)
                s_new = s_new + lax.dot_general(kd, v, _TN, preferred_element_type=F32)
                o_ref[r, :, vsl] = (_rms(y) * jax.nn.silu(g_ref[r, :, vsl])).astype(BF16)
            state[r, pair] = s_new


def _retention(q, k, v, gate):
    b, lp, _ = q.shape
    t = RET_TILE
    log_g = np.log(1.0 - 2.0 ** (-5.0 - np.arange(B_HEADS)))
    idx = np.arange(t, dtype=np.float64)
    rel = idx[:, None] - idx[None, :]
    dmat = np.where(rel >= 0, np.exp(np.maximum(rel, 0.0)[None] * log_g[:, None, None]), 0.0)
    qdec = np.exp((idx + 1.0)[None, :, None] * log_g[:, None, None])
    kdec = np.exp((t - 1.0 - idx)[None, :, None] * log_g[:, None, None])
    cdec = np.repeat(np.exp(t * log_g), HEAD_DIM).reshape(B_HEADS // 2, LANES, 1)
    dmat, qdec, kdec, cdec = (jnp.asarray(a, F32) for a in (dmat, qdec, kdec, cdec))

    qk_w, v_w = B_HEADS * HEAD_DIM, B_HEADS * B_VDIM
    nb = MIX_BATCH
    seq = lambda n: pl.BlockSpec((nb, t, n), lambda bi, i: (bi, i, 0))
    return pl.pallas_call(
        _retention_kernel,
        grid=(b // nb, lp // t),
        in_specs=[seq(qk_w), seq(qk_w), seq(v_w), seq(v_w),
                  _resident((B_HEADS, t, t)), _resident((B_HEADS, t, 1)),
                  _resident((B_HEADS, t, 1)), _resident((B_HEADS // 2, LANES, 1))],
        out_specs=seq(v_w),
        out_shape=jax.ShapeDtypeStruct((b, lp, v_w), BF16),
        scratch_shapes=[pltpu.VMEM((nb, B_HEADS // 2, LANES, B_VDIM), F32)],
        compiler_params=_params(("parallel", "arbitrary")),
        name="retention",
    )(q, k, v, gate, dmat, qdec, kdec, cdec)


def _swa_kernel(q_ref, k_ref, v_ref, sink_ref, o_ref):
    j = pl.program_id(1)
    nb, tq = q_ref.shape[0], q_ref.shape[1]
    lp = k_ref.shape[1]
    win = tq + C_WIN_CHUNKS * CHUNK
    rep = C_HEADS // C_KV_HEADS
    low_half = _low_half()

    start = pl.multiple_of(jnp.clip(j * tq - C_WIN_CHUNKS * CHUNK, 0, lp - win), CHUNK)
    q_row = j * tq + lax.broadcasted_iota(jnp.int32, (tq, 1), 0)
    q_col = j * tq + lax.broadcasted_iota(jnp.int32, (1, tq), 1)
    q_chunk = jnp.concatenate([q_col // CHUNK] * rep, axis=1)
    key = lax.broadcasted_iota(jnp.int32, (CHUNK + win, 1), 0)
    k_chunk = start // CHUNK + (key - CHUNK) // CHUNK
    in_window = ((key >= CHUNK) & (k_chunk >= 1) & (k_chunk <= q_chunk)
                 & (k_chunk >= q_chunk - C_WIN_CHUNKS))
    valid = in_window | ((key < CHUNK) & (key >= PAD))
    v_row_low = lax.broadcasted_iota(jnp.int32, (LANES, 1), 0) < HEAD_DIM

    chains = [(r, g) for r in range(nb) for g in range(C_KV_HEADS)]
    sinks, scores, values = [], [], []
    for r, g in chains:
        q = jnp.where(q_row < lp, q_ref[r], jnp.zeros_like(q_ref[r]))
        sl = slice(g * LANES, (g + 1) * LANES)
        keys = jnp.concatenate([k_ref[r, 0:CHUNK, sl], k_ref[r, pl.ds(start, win), sl]], axis=0)
        values.append(jnp.concatenate([v_ref[r, 0:CHUNK, sl], v_ref[r, pl.ds(start, win), sl]], axis=0))
        heads = range(g * rep, (g + 1) * rep)
        qs = jnp.concatenate(
            [_keep_head(q[:, (h // 2) * LANES:(h // 2 + 1) * LANES], h % 2, low_half)
             for h in heads], axis=0)
        sinks.append(jnp.concatenate(
            [jnp.broadcast_to(sink_ref[0:1, h:h + 1] * LOG2E, (1, tq)) for h in heads], axis=1))
        scores.append(jnp.where(valid, lax.dot_general(keys, qs, _NT, preferred_element_type=F32), NEG))
    for (r, g), sink, s, vals in zip(chains, sinks, scores, values):
        m = jnp.maximum(jnp.max(s, axis=0, keepdims=True), sink)
        p = jnp.exp2(s - m)
        denom = jnp.sum(p, axis=0, keepdims=True) + jnp.exp2(sink - m)
        y_t = lax.dot_general(vals, p.astype(BF16), _TN, preferred_element_type=F32) / denom
        for u in range(rep // 2):
            out_t = jnp.where(v_row_low, y_t[:, 2 * u * tq:(2 * u + 1) * tq],
                              y_t[:, (2 * u + 1) * tq:(2 * u + 2) * tq])
            pair = g * rep // 2 + u
            o_ref[r, :, pair * LANES:(pair + 1) * LANES] = jnp.where(
                q_row >= PAD, out_t.T, 0.0).astype(BF16)


def _swa(q, k, v, sinks):
    b, lp, width = q.shape
    tq = KEY_TILE
    nb = SWA_BATCH
    kv = pl.BlockSpec((nb, lp, k.shape[2]), lambda bi, j: (bi, 0, 0))
    return pl.pallas_call(
        _swa_kernel,
        grid=(b // nb, -(-lp // tq)),
        in_specs=[pl.BlockSpec((nb, tq, width), lambda bi, j: (bi, j, 0)), kv, kv,
                  _resident((1, C_HEADS))],
        out_specs=pl.BlockSpec((nb, tq, width), lambda bi, j: (bi, j, 0)),
        out_shape=jax.ShapeDtypeStruct((b, lp, width), BF16),
        compiler_params=_params(("parallel", "parallel")),
        name="swa_sinks",
    )(q, k, v, sinks.reshape(1, C_HEADS))


def _pool_kernel(x_ref, mix_ref, scale_ref, o_ref):
    lp = x_ref.shape[1]
    row = lax.broadcasted_iota(jnp.int32, (lp, 1), 0)
    t = row - PAD
    outs = []
    for gi, w in enumerate(POOL_WINDOWS):
        x = x_ref[0, :, gi * D_GROUP:(gi + 1) * D_GROUP]
        s, shift = x, 1
        while shift < w:
            s = s + jnp.where(row >= shift, pltpu.roll(s, shift, 0), 0.0)
            shift *= 2
        count = jnp.maximum(jnp.minimum(t + 1, w), 1).astype(F32)
        y = (s / count - x).astype(BF16)
        outs.append(jnp.dot(y, mix_ref[gi].astype(BF16), preferred_element_type=F32))
    out = jnp.concatenate(outs, axis=-1) * scale_ref[...]
    o_ref[0] = jnp.where(row >= PAD, out, 0.0).astype(BF16)


def _pool(x, mix, scale):
    b, lp, width = x.shape
    n_groups = len(POOL_WINDOWS)
    return pl.pallas_call(
        _pool_kernel,
        grid=(b,),
        in_specs=[pl.BlockSpec((1, lp, width), lambda bi: (bi, 0, 0)),
                  _resident((n_groups, D_GROUP, D_GROUP)), _resident((1, width))],
        out_specs=pl.BlockSpec((1, lp, width), lambda bi: (bi, 0, 0)),
        out_shape=jax.ShapeDtypeStruct((b, lp, width), BF16),
        compiler_params=_params(("parallel",)),
        name="pool_mixer",
    )(x, mix, scale.reshape(1, width))


def kernel(x, meta_tokens, ffn1_norm, ffn1_w_in, ffn1_w_out, mix_norm, ffn2_norm, ffn2_w_in,
           ffn2_w_out, ev_w_in, ev_a_q_norm, ev_a_k_norm, ev_w_out, od_w_in, od_c_q_norm,
           od_c_k_norm, od_c_sinks, od_d_mix, od_d_scale, od_w_out):
    b, s, d = x.shape
    lp = PAD + N_META + s
    assert d == D_MODEL and (b * lp) % ROW_TILE == 0 and s % ROW_TILE == 0
    assert lp % PROJ_TILE == 0 and lp % RET_TILE == 0
    assert min(TOP_K, s // 4) == TOP_K and b % DSA_BATCH == 0 and b % MIX_BATCH == 0 and b % SWA_BATCH == 0
    depth = ffn1_norm.shape[0]
    seq = lambda a: a.reshape(b, lp, a.shape[-1])
    flat = lambda a: a.reshape(b * lp, a.shape[-1])

    ffn1_w_in, ffn1_w_out, ffn2_w_in, ffn2_w_out = (
        w.astype(BF16) for w in (ffn1_w_in, ffn1_w_out, ffn2_w_in, ffn2_w_out))
    for layer in range(depth):
        if layer == 0:
            h = _ffn_first(x, meta_tokens, ffn1_norm[0], ffn1_w_in, ffn1_w_out, 0)
        else:
            h = _ffn(h, ffn1_norm[layer], ffn1_w_in, ffn1_w_out, layer)
        if layer % 2 == 0:
            e = layer // 2
            aq, ak, av, iq, ik, iw, bq, bk, bv, bg = _even_proj(
                seq(h), mix_norm[layer], ev_w_in[e], ev_a_q_norm[e], ev_a_k_norm[e])
            ya = _dsa(aq, ak, av, iq, ik, iw)
            yb = _retention(bq, bk, bv, bg)
            mix = (flat(ya), flat(yb), ev_w_out[e].astype(BF16))
        else:
            o = layer // 2
            cq, ck, cv, dx = _odd_proj(seq(h), mix_norm[layer], od_w_in[o], od_c_q_norm[o],
                                       od_c_k_norm[o])
            yc = _swa(cq, ck, cv, od_c_sinks[o])
            yd = _pool(dx, od_d_mix[o], od_d_scale[o])
            mix = (flat(yc), flat(yd), od_w_out[o].astype(BF16))
        last = layer == depth - 1
        h = _ffn(h, ffn2_norm[layer], ffn2_w_in, ffn2_w_out, layer, (b, s) if last else None, mix)
    return h
```

```python
import functools

import jax
import jax.numpy as jnp
import numpy as np
from jax import lax
from jax.experimental import pallas as pl
from jax.experimental.pallas import tpu as pltpu

F32 = jnp.float32
BF16 = jnp.bfloat16

D_MODEL = 1024
D_FF = 2816
CHUNK = 64
N_META = 16
PAD = CHUNK - N_META
HEAD_DIM = 64
LANES = 128
A_HEADS = 8
IDX_HEADS = 8
IDX_DIM = 64
TOP_K = 256
B_HEADS = 8
B_VDIM = 128
C_HEADS = 8
C_KV_HEADS = 2
C_WIN_CHUNKS = 2
D_WIDTH = 512
POOL_WINDOWS = (2, 4, 8, 16)
D_GROUP = 128
EPS = 1e-6
NEG = -1e30
BIG = 3e38
LOG2E = 1.4426950408889634

VMEM_LIMIT = 56 * 1024 * 1024
ROW_TILE = 512
FF_CHUNK = 256
PROJ_TILE = 704
KEY_TILE = 128
KEY_UNROLL = 4
DSA_BATCH = 2
MIX_BATCH = 2
SWA_BATCH = 8
RET_TILE = 528
BISECT_ITERS = 15

_NT = (((1,), (1,)), ((), ()))
_TN = (((0,), (0,)), ((), ()))


def _params(sem):
    return pltpu.CompilerParams(dimension_semantics=sem, vmem_limit_bytes=VMEM_LIMIT)


def _rms(x, gain=None):
    y = x * lax.rsqrt(jnp.mean(x * x, axis=-1, keepdims=True) + EPS)
    return y if gain is None else y * gain


def _resident(shape):
    return pl.BlockSpec(shape, lambda *_: (0,) * len(shape), pipeline_mode=pl.Buffered(1))


def _low_half():
    return lax.broadcasted_iota(jnp.int32, (1, LANES), 1) < HEAD_DIM


def _keep_head(x, half, low_half):
    return jnp.where(low_half if half == 0 else jnp.logical_not(low_half), x, jnp.zeros_like(x))


def _ffn_kernel(h_ref, g_ref, win_ref, wout_ref, o_ref, mix_refs=None):
    x = h_ref[...]
    if mix_refs is not None:
        ya_ref, yb_ref, wmix_ref = mix_refs
        na = ya_ref.shape[1]
        x = x + jnp.dot(ya_ref[...], wmix_ref[0:na, :], preferred_element_type=F32)
        x = x + jnp.dot(yb_ref[...], wmix_ref[na:, :], preferred_element_type=F32)
    xn = _rms(x, g_ref[...]).astype(BF16)
    acc = jnp.zeros(x.shape, F32)
    for c in range(D_FF // FF_CHUNK):
        lo = c * FF_CHUNK
        g = jnp.dot(xn, win_ref[:, lo:lo + FF_CHUNK], preferred_element_type=F32)
        u = jnp.dot(xn, win_ref[:, D_FF + lo:D_FF + lo + FF_CHUNK], preferred_element_type=F32)
        a = (jax.nn.silu(g) * u).astype(BF16)
        acc = acc + jnp.dot(a, wout_ref[lo:lo + FF_CHUNK, :], preferred_element_type=F32)
    o_ref[...] = x + 0.5 * acc


def _mix_ffn_kernel(h_ref, ya_ref, yb_ref, wmix_ref, g_ref, win_ref, wout_ref, o_ref):
    _ffn_kernel(h_ref, g_ref, win_ref, wout_ref, o_ref, (ya_ref, yb_ref, wmix_ref))


def _layer_weights(shape, layer):
    return pl.BlockSpec((None,) + shape, lambda *_: (layer, 0, 0), pipeline_mode=pl.Buffered(1))


def _ffn_weight_specs(layer):
    return [_resident((1, D_MODEL)), _layer_weights((D_MODEL, 2 * D_FF), layer),
            _layer_weights((D_FF, D_MODEL), layer)]


def _frame_rows(b, lp, s, n):
    return pl.BlockSpec(
        (pl.Element(ROW_TILE), pl.Element(n)),
        lambda bi, t: (pl.multiple_of(bi * lp + lp - s + ROW_TILE * t, CHUNK), 0))


def _ffn(h, gain, w_in, w_out, layer, frames_out=None, mix=None):
    if frames_out is None:
        rows = h.shape[0]
        grid = (rows // ROW_TILE,)
        row_spec = lambda n: pl.BlockSpec((ROW_TILE, n), lambda i: (i, 0))
        out_spec = row_spec(D_MODEL)
        out_shape = jax.ShapeDtypeStruct((rows, D_MODEL), F32)
        sem = ("parallel",)
    else:
        b, s = frames_out
        grid = (b, s // ROW_TILE)
        row_spec = functools.partial(_frame_rows, b, h.shape[0] // b, s)
        out_spec = pl.BlockSpec((None, ROW_TILE, D_MODEL), lambda bi, t: (bi, t, 0))
        out_shape = jax.ShapeDtypeStruct((b, s, D_MODEL), F32)
        sem = ("parallel", "parallel")
    if mix is None:
        body, rows_in, specs = _ffn_kernel, [h], [row_spec(D_MODEL)]
    else:
        ya, yb, w_mix = mix
        body, rows_in = _mix_ffn_kernel, [h, ya, yb, w_mix]
        specs = [row_spec(D_MODEL), row_spec(ya.shape[1]), row_spec(yb.shape[1]),
                 _resident(w_mix.shape)]
    return pl.pallas_call(
        body,
        grid=grid,
        in_specs=specs + _ffn_weight_specs(layer),
        out_specs=out_spec,
        out_shape=out_shape,
        compiler_params=_params(sem),
        name="ffn",
    )(*rows_in, gain.reshape(1, D_MODEL), w_in, w_out)


def _ffn_meta_kernel(c_ref, g_ref, win_ref, wout_ref, h_hbm_ref, o_ref, y_s):
    @pl.when(pl.program_id(0) == 0)
    def _():
        _ffn_kernel(c_ref, g_ref, win_ref, wout_ref, y_s)

    o_ref[...] = y_s[...]


def _ffn_first(x, meta_tokens, gain, w_in, w_out, layer):
    b, s, _ = x.shape
    lp = PAD + N_META + s
    tiles = s // ROW_TILE
    gain = gain.reshape(1, D_MODEL)
    h = pl.pallas_call(
        _ffn_kernel,
        grid=(b, tiles),
        in_specs=[pl.BlockSpec((ROW_TILE, D_MODEL), lambda bi, t: (bi * tiles + t, 0))]
        + _ffn_weight_specs(layer),
        out_specs=_frame_rows(b, lp, s, D_MODEL),
        out_shape=jax.ShapeDtypeStruct((b * lp, D_MODEL), F32),
        compiler_params=_params(("parallel", "parallel")),
        name="ffn_frames",
    )(x.reshape(b * s, D_MODEL), gain, w_in, w_out)
    chunk0 = jnp.concatenate([jnp.zeros((PAD, D_MODEL), x.dtype), meta_tokens.astype(x.dtype)], axis=0)
    return pl.pallas_call(
        _ffn_meta_kernel,
        grid=(b,),
        in_specs=[_resident((CHUNK, D_MODEL))] + _ffn_weight_specs(layer)
        + [pl.BlockSpec(memory_space=pl.ANY)],
        out_specs=pl.BlockSpec((CHUNK, D_MODEL), lambda bi: (bi * (lp // CHUNK), 0)),
        out_shape=jax.ShapeDtypeStruct((b * lp, D_MODEL), F32),
        scratch_shapes=[pltpu.VMEM((CHUNK, D_MODEL), F32)],
        input_output_aliases={4: 0},
        compiler_params=_params(("arbitrary",)),
        name="ffn_meta",
    )(chunk0, gain, w_in, w_out, h)


def _head_norm(y, seg_ref, gain):
    sq = (y * y).astype(BF16)
    ms = jnp.concatenate(
        [jnp.dot(sq[:, p * LANES:(p + 1) * LANES], seg_ref[...], preferred_element_type=F32)
         for p in range(y.shape[1] // LANES)], axis=-1)
    return y * lax.rsqrt(ms + EPS) * gain


def _rotate_half(x, first_half):
    n = x.shape[-1]
    half = HEAD_DIM // 2
    return jnp.where(first_half, -pltpu.roll(x, n - half, 1), pltpu.roll(x, half, 1))


def _column_groups(xn, w_refs):
    state = [0, 0]

    def cols(n):
        if state[1] == w_refs[state[0]].shape[1]:
            state[:] = [state[0] + 1, 0]
        w_ref, off = w_refs[state[0]], state[1]
        state[1] += n
        return jnp.dot(xn, w_ref[:, off:off + n], preferred_element_type=F32)

    return cols


def _even_proj_kernel(h_ref, g_ref, wa_ref, wi_ref, wb_ref, seg_ref, qg_ref, kg_ref, cos_ref, sin_ref,
                      aq_ref, ak_ref, av_ref, iq_ref, ik_ref, iw_ref, bq_ref, bk_ref, bv_ref,
                      bg_ref):
    xn = _rms(h_ref[0], g_ref[...]).astype(BF16)
    cols = _column_groups(xn, (wa_ref, wi_ref, wb_ref))

    w = A_HEADS * HEAD_DIM
    aq_ref[0] = _head_norm(cols(w), seg_ref, qg_ref[...]).astype(BF16)
    ak_ref[0] = _head_norm(cols(w), seg_ref, kg_ref[...]).astype(BF16)
    av_ref[0] = cols(w).astype(BF16)
    iq_ref[0] = cols(IDX_HEADS * IDX_DIM).astype(BF16)
    ik_ref[0] = cols(LANES).astype(BF16)
    iw_ref[0] = cols(LANES) * (IDX_HEADS ** -0.5 * IDX_DIM ** -0.5)
    cos, sin = cos_ref[...], sin_ref[...]
    first_half = lax.broadcasted_iota(jnp.int32, (1, w), 1) % HEAD_DIM < HEAD_DIM // 2
    q = cols(w)
    bq_ref[0] = (q * cos + _rotate_half(q, first_half) * sin).astype(BF16)
    k = cols(w)
    bk_ref[0] = ((k * cos + _rotate_half(k, first_half) * sin) * HEAD_DIM ** -0.5).astype(BF16)
    for c in range(2):
        sl = slice(c * w, (c + 1) * w)
        bv_ref[0, :, sl] = cols(w).astype(BF16)
    for c in range(2):
        sl = slice(c * w, (c + 1) * w)
        bg_ref[0, :, sl] = cols(w)


def _odd_proj_kernel(h_ref, g_ref, w_ref, seg_ref, qg_ref, kg_ref, cq_ref, ck_ref, cv_ref, dx_ref):
    xn = _rms(h_ref[0], g_ref[...]).astype(BF16)
    cols = _column_groups(xn, (w_ref,))

    kv_w = 2 * C_KV_HEADS * HEAD_DIM
    q, k, v, dx = cols(C_HEADS * HEAD_DIM), cols(kv_w), cols(kv_w), cols(D_WIDTH)
    cq_ref[0] = _head_norm(q, seg_ref, qg_ref[...]).astype(BF16)
    ck_ref[0] = _head_norm(k, seg_ref, kg_ref[...]).astype(BF16)
    cv_ref[0] = v.astype(BF16)
    dx_ref[0] = dx


def _seg_matrix():
    g = jnp.arange(LANES) // HEAD_DIM
    return jnp.where(g[:, None] == g[None, :], 1.0 / HEAD_DIM, 0.0).astype(BF16)


def _proj_call(body, name, h, gain, weights, extra_in, extra_specs, outs):
    b, lp, _ = h.shape
    t = PROJ_TILE
    seq = lambda n: pl.BlockSpec((1, t, n), lambda bi, i: (bi, i, 0))
    return pl.pallas_call(
        body,
        grid=(b, lp // t),
        in_specs=[seq(D_MODEL), _resident((1, D_MODEL))]
        + [_resident((w.shape[0], n)) for w, n in weights]
        + [_resident((LANES, LANES))] + extra_specs,
        out_specs=[seq(n) for n, _ in outs],
        out_shape=[jax.ShapeDtypeStruct((b, lp, n), dt) for n, dt in outs],
        compiler_params=_params(("parallel", "parallel")),
        name=name,
    )(h, gain.reshape(1, D_MODEL), *[w for w, _ in weights], _seg_matrix(), *extra_in)


def _rotary_tables(lp):
    pos = np.arange(lp, dtype=np.float64) - PAD
    inv = 1.0 / (10000.0 ** (np.arange(0, HEAD_DIM, 2, dtype=np.float64) / HEAD_DIM))
    ang = pos[:, None] * inv[None]
    return (jnp.asarray(np.tile(np.cos(ang), (1, 2 * B_HEADS)), F32),
            jnp.asarray(np.tile(np.sin(ang), (1, 2 * B_HEADS)), F32))


def _even_proj(h, gain, w, q_gain, k_gain):
    lp = h.shape[1]
    w = w.astype(BF16)
    a_w = 3 * A_HEADS * HEAD_DIM + IDX_HEADS * IDX_DIM
    ik = w[:, a_w:a_w + IDX_DIM]
    iw = jnp.pad(w[:, a_w + IDX_DIM:a_w + IDX_DIM + IDX_HEADS], ((0, 0), (0, LANES - IDX_HEADS)))
    weights = [(w, a_w), (jnp.concatenate([ik, ik, iw], axis=1), 2 * LANES),
               (w[:, a_w + IDX_DIM + IDX_HEADS:], w.shape[1] - a_w - IDX_DIM - IDX_HEADS)]
    cos, sin = _rotary_tables(lp)
    w_heads = A_HEADS * HEAD_DIM
    qg = jnp.tile(q_gain, A_HEADS).reshape(1, w_heads) * (HEAD_DIM ** -0.5 * LOG2E)
    kg = jnp.tile(k_gain, A_HEADS).reshape(1, w_heads)
    table = pl.BlockSpec((PROJ_TILE, w_heads), lambda bi, i: (i, 0))
    outs = [(w_heads, BF16)] * 4 + [(LANES, BF16), (LANES, F32), (w_heads, BF16), (w_heads, BF16),
                                    (B_HEADS * B_VDIM, BF16), (B_HEADS * B_VDIM, F32)]
    return _proj_call(_even_proj_kernel, "even_in_proj", h, gain, weights,
                      [qg, kg, cos, sin],
                      [_resident((1, w_heads)), _resident((1, w_heads)), table, table], outs)


def _odd_proj(h, gain, w, q_gain, k_gain):
    w = w.astype(BF16)
    q_w = C_HEADS * HEAD_DIM
    kv_w = C_KV_HEADS * HEAD_DIM
    dup = lambda m: jnp.concatenate(
        [m[:, g * HEAD_DIM:(g + 1) * HEAD_DIM] for g in range(C_KV_HEADS) for _ in range(2)], axis=1)
    w = jnp.concatenate([w[:, :q_w], dup(w[:, q_w:q_w + kv_w]), dup(w[:, q_w + kv_w:q_w + 2 * kv_w]),
                         w[:, q_w + 2 * kv_w:]], axis=1)
    qg = jnp.tile(q_gain, C_HEADS).reshape(1, q_w) * (HEAD_DIM ** -0.5 * LOG2E)
    kg = jnp.tile(k_gain, 2 * C_KV_HEADS).reshape(1, 2 * kv_w)
    outs = [(q_w, BF16), (2 * kv_w, BF16), (2 * kv_w, BF16), (D_WIDTH, F32)]
    return _proj_call(_odd_proj_kernel, "odd_in_proj", h, gain, [(w, w.shape[1])], [qg, kg],
                      [_resident((1, q_w)), _resident((1, 2 * kv_w))], outs)


def _fold(x):
    return x.reshape(x.shape[0] // 8, 8, x.shape[1]).sum(axis=0)


def _any(mask):
    return jnp.max(jnp.where(mask, 1.0, 0.0)) > 0.0


def _topk_threshold(sc_s, n_groups, lo, hi, search):
    nq = sc_s.shape[2]

    def scan(fn, init):
        def body(g, acc):
            for u in range(KEY_UNROLL):
                c = g * KEY_UNROLL + u
                acc = fn(acc, sc_s[c], c)
            return acc
        return lax.fori_loop(0, n_groups, body, init)

    def count(pred):
        acc = scan(lambda a, s, c: a + _fold(jnp.where(pred(s, c), 1.0, 0.0)), jnp.zeros((8, nq), F32))
        return jnp.sum(acc, axis=0, keepdims=True)

    def step(lo, hi, mid):
        up = count(lambda s, c: s > mid) >= TOP_K
        return jnp.where(up, mid, lo), jnp.where(up, hi, mid)

    def coarse(_, c):
        lo, hi = c
        return step(lo, hi, 0.5 * (lo + hi))

    lo, hi = lax.fori_loop(0, BISECT_ITERS, coarse, (lo, hi))

    def bounds(lo, hi):
        def fn(acc, s, c):
            t_lo, t_hi = acc
            t_lo = jnp.minimum(t_lo, jnp.min(jnp.where(s > lo, s, BIG), axis=0, keepdims=True))
            t_hi = jnp.maximum(t_hi, jnp.max(jnp.where(s <= hi, s, -BIG), axis=0, keepdims=True))
            return t_lo, t_hi
        return scan(fn, (jnp.full((1, nq), BIG, F32), jnp.full((1, nq), -BIG, F32)))

    def unfinished(c):
        _, _, t_lo, t_hi = c
        return _any((t_lo != t_hi) & search)

    def refine(c):
        lo, hi, t_lo, t_hi = c
        mid = 0.5 * (t_lo + t_hi)
        mid = jnp.where(mid >= t_hi, t_lo, mid)
        lo, hi = step(lo, hi, mid)
        return (lo, hi) + bounds(lo, hi)

    _, _, _, thr = lax.while_loop(unfinished, refine, (lo, hi) + bounds(lo, hi))
    return thr, TOP_K - count(lambda s, c: s > thr)


def _dsa_kernel(aq_ref, iq_ref, iw_ref, ak_ref, av_ref, ik_ref, o_ref, k_s, vt_s, ik_s, sc_s, acc_s):
    j = pl.program_id(1)
    nb = aq_ref.shape[0]
    lp = ak_ref.shape[1]
    kt = KEY_TILE
    n_full, tail = divmod(lp, kt)
    pairs = A_HEADS // 2
    rows = range(nb)

    @pl.when(j == 0)
    def _():
        def put(r, c, k, ik, v):
            k_s[r, c] = k
            ik_s[r, c] = ik
            v = v.astype(F32)
            for p in range(pairs):
                vt_s[r, c, p] = v[:, p * LANES:(p + 1) * LANES].T.astype(BF16)

        def copy_tile(c, carry):
            at = pl.ds(pl.multiple_of(c * kt, kt), kt)
            for r in rows:
                put(r, c, ak_ref[r, at, :], ik_ref[r, at, :], av_ref[r, at, :])
            return carry

        lax.fori_loop(0, n_full, copy_tile, 0)
        padded = lambda a: jnp.concatenate([a, jnp.zeros((kt - tail, a.shape[1]), a.dtype)], axis=0)
        at = slice(n_full * kt, lp)
        zeros = lambda ref: jnp.zeros(ref.shape[2:], BF16)
        for r in rows:
            if tail:
                put(r, n_full, padded(ak_ref[r, at, :]), padded(ik_ref[r, at, :]),
                    padded(av_ref[r, at, :]))
            for c in range(n_full + (tail > 0), k_s.shape[1]):
                put(r, c, zeros(k_s), zeros(ik_s), zeros(k_s))
        sc_s[...] = jnp.full(sc_s.shape, NEG, F32)

    low_half = _low_half()
    q_row = j * kt + lax.broadcasted_iota(jnp.int32, (kt, 1), 0)
    q_col = j * kt + lax.broadcasted_iota(jnp.int32, (1, kt), 1)
    in_range = q_row < lp

    def both_heads(x, p):
        x = x[:, p * LANES:(p + 1) * LANES]
        return jnp.concatenate([_keep_head(x, 0, low_half), _keep_head(x, 1, low_half)], axis=0)

    aq_p, iq_p, iw_t = [], [], []
    for r in rows:
        aq = jnp.where(in_range, aq_ref[r], jnp.zeros_like(aq_ref[r]))
        iq = jnp.where(in_range, iq_ref[r], jnp.zeros_like(iq_ref[r]))
        aq_p.append([both_heads(aq, p) for p in range(pairs)])
        iq_p.append([both_heads(iq, p) for p in range(pairs)])
        iw_t.append(jnp.where(in_range, iw_ref[r], 0.0).T)
    key_grid = lax.broadcasted_iota(jnp.int32, (kt, kt), 0)
    key_limit = jnp.minimum((q_col // CHUNK + 1) * CHUNK, lp)
    n_groups = (j + KEY_UNROLL) // KEY_UNROLL
    cols = lambda r: slice(r * kt, (r + 1) * kt)

    def score_tiles(g, carry):
        mn, mx = carry
        tiles = [g * KEY_UNROLL + u for u in range(KEY_UNROLL)]
        logits = [[[lax.dot_general(ik_s[r, c], iq_p[r][p], _NT, preferred_element_type=F32)
                    for p in range(pairs)] for r in rows] for c in tiles]
        mn, mx = list(mn), list(mx)
        for c, per_row in zip(tiles, logits):
            key = key_grid + c * kt
            admissible = (key >= jnp.where(c == 0, PAD, 0)) & (key < key_limit)
            for r, lg in zip(rows, per_row):
                sc = jnp.zeros((kt, kt), F32)
                for p in range(pairs):
                    for half in range(2):
                        h = 2 * p + half
                        sc = sc + iw_t[r][h:h + 1, :] * jnp.maximum(
                            lg[p][:, half * kt:(half + 1) * kt], 0.0)
                sc_s[c, :, cols(r)] = jnp.where(admissible, sc, NEG)
                mn[r] = jnp.minimum(mn[r], jnp.min(jnp.where(admissible, sc, BIG), axis=0, keepdims=True))
                mx[r] = jnp.maximum(mx[r], jnp.max(jnp.where(admissible, sc, NEG), axis=0, keepdims=True))
        return tuple(mn), tuple(mx)

    mn, mx = lax.fori_loop(0, n_groups, score_tiles,
                           (tuple(jnp.full((1, kt), BIG, F32) for _ in rows),
                            tuple(jnp.full((1, kt), NEG, F32) for _ in rows)))
    mn = jnp.concatenate(mn, axis=1)
    mx = jnp.concatenate(mx, axis=1)

    first_search_block = (TOP_K - N_META + CHUNK - 1) // CHUNK * CHUNK // kt

    tri = (lax.broadcasted_iota(jnp.int32, (kt, kt), 0)
           >= lax.broadcasted_iota(jnp.int32, (kt, kt), 1)).astype(BF16)

    def to_bias(thr, n_equal):
        def body(g, seen):
            for u in range(KEY_UNROLL):
                c = g * KEY_UNROLL + u
                s = sc_s[c]
                equal = s == thr
                rank = seen + jnp.dot(tri, jnp.where(equal, 1.0, 0.0).astype(BF16),
                                      preferred_element_type=F32)
                sc_s[c] = jnp.where((s > thr) | (equal & (rank <= n_equal)), 0.0, NEG)
                seen = rank[kt - 1:kt, :]
            return seen
        lax.fori_loop(0, n_groups, body, jnp.zeros((1, nb * kt), F32))

    @pl.when(j < first_search_block)
    def _():
        to_bias(jnp.full((1, nb * kt), 0.5 * NEG, F32), jnp.zeros((1, nb * kt), F32))

    @pl.when(j >= first_search_block)
    def _():
        search = jnp.concatenate([q_col < lp] * nb, axis=1)
        to_bias(*_topk_threshold(sc_s, n_groups, mn - (jnp.abs(mn) * 2.0 ** -10 + 1e-30), mx, search))

    acc_s[...] = jnp.zeros(acc_s.shape, F32)
    chains = [(r, p) for r in rows for p in range(pairs)]
    ones = jnp.ones((8, KEY_UNROLL * kt), BF16)

    def attend(g, carry):
        m, l = carry
        c0 = g * KEY_UNROLL
        s = []
        for r in rows:
            bias = jnp.concatenate([sc_s[c0 + u, :, cols(r)] for u in range(KEY_UNROLL)], axis=0)
            bias = jnp.concatenate([bias, bias], axis=1)
            for p in range(pairs):
                k = jnp.concatenate([k_s[r, c0 + u, :, p * LANES:(p + 1) * LANES]
                                     for u in range(KEY_UNROLL)], axis=0)
                s.append(lax.dot_general(k, aq_p[r][p], _NT, preferred_element_type=F32) + bias)
        new_m, new_l = [], []
        for i, (r, p) in enumerate(chains):
            m_i = jnp.maximum(m[i], jnp.max(s[i], axis=0, keepdims=True))
            alpha = jnp.exp2(m[i] - m_i)
            pr = jnp.exp2((s[i] - m_i).astype(BF16))
            new_l.append(alpha * l[i] + jnp.dot(ones, pr, preferred_element_type=F32)[0:1])
            vt = jnp.concatenate([vt_s[r, c0 + u, p] for u in range(KEY_UNROLL)], axis=1)
            acc_s[r, p] = alpha * acc_s[r, p] + jnp.dot(vt, pr, preferred_element_type=F32)
            new_m.append(m_i)
        return tuple(new_m), tuple(new_l)

    init = (tuple(jnp.full((1, 2 * kt), -BIG, F32) for _ in chains),
            tuple(jnp.zeros((1, 2 * kt), F32) for _ in chains))
    _, l = lax.fori_loop(0, n_groups, attend, init)

    v_row_low = lax.broadcasted_iota(jnp.int32, (LANES, 1), 0) < HEAD_DIM
    for i, (r, p) in enumerate(chains):
        out_t = acc_s[r, p] / l[i]
        out_t = jnp.where(v_row_low, out_t[:, 0:kt], out_t[:, kt:2 * kt])
        o_ref[r, :, p * LANES:(p + 1) * LANES] = jnp.where(q_row >= PAD, out_t.T, 0.0).astype(BF16)


def _dsa(aq, ak, av, iq, ik, iw):
    b, lp, width = aq.shape
    kt = KEY_TILE
    nb = DSA_BATCH
    n_tiles = -(-lp // kt)
    n_scan = -(-n_tiles // KEY_UNROLL) * KEY_UNROLL
    q_spec = lambda n: pl.BlockSpec((nb, kt, n), lambda bi, j: (bi, j, 0))
    full_spec = lambda n: pl.BlockSpec((nb, lp, n), lambda bi, j: (bi, 0, 0))
    return pl.pallas_call(
        _dsa_kernel,
        grid=(b // nb, n_tiles),
        in_specs=[q_spec(width), q_spec(width), q_spec(LANES), full_spec(width), full_spec(width),
                  full_spec(LANES)],
        out_specs=q_spec(width),
        out_shape=jax.ShapeDtypeStruct((b, lp, width), BF16),
        scratch_shapes=[
            pltpu.VMEM((nb, n_scan, kt, width), BF16),
            pltpu.VMEM((nb, n_scan, A_HEADS // 2, LANES, kt), BF16),
            pltpu.VMEM((nb, n_scan, kt, LANES), BF16),
            pltpu.VMEM((n_scan, kt, nb * kt), F32),
            pltpu.VMEM((nb, A_HEADS // 2, LANES, 2 * kt), F32),
        ],
        compiler_params=_params(("parallel", "arbitrary")),
        name="dsa_attention",
    )(aq, iq, iw, ak, av, ik)


def _retention_kernel(q_ref, k_ref, v_ref, g_ref, dmat_ref, qdec_ref, kdec_ref, cdec_ref, o_ref,
                      state):
    i = pl.program_id(1)

    @pl.when(i == 0)
    def _():
        state[...] = jnp.zeros(state.shape, F32)

    low_half = _low_half()
    for pair in range(B_HEADS // 2):
        sl = slice(pair * LANES, (pair + 1) * LANES)
        for r in range(q_ref.shape[0]):
            q_pair = q_ref[r, :, sl]
            k_pair = k_ref[r, :, sl]
            s_old = state[r, pair]
            s_new = s_old * cdec_ref[pair]
            for half in range(2):
                h = 2 * pair + half
                vsl = slice(h * B_VDIM, (h + 1) * B_VDIM)
                v = v_ref[r, :, vsl]
                q = _keep_head(q_pair, half, low_half)
                inner = lax.dot_general(q, k_pair, _NT, preferred_element_type=F32) * dmat_ref[h]
                y = jnp.dot(inner.astype(BF16), v, preferred_element_type=F32)
                y = y + jnp.dot(q, s_old.astype(BF16), preferred_element_type=F32) * qdec_ref[h]
                kd = _keep_head((k_pair.astype(F32) * kdec_ref[h]).astype(BF16), half, low_half)
                s_new = s_new + lax.dot_general(kd, v, _TN, preferred_element_type=F32)
                o_ref[r, :, vsl] = (_rms(y) * jax.nn.silu(g_ref[r, :, vsl])).astype(BF16)
            state[r, pair] = s_new


def _retention(q, k, v, gate):
    b, lp, _ = q.shape
    t = RET_TILE
    log_g = np.log(1.0 - 2.0 ** (-5.0 - np.arange(B_HEADS)))
    idx = np.arange(t, dtype=np.float64)
    rel = idx[:, None] - idx[None, :]
    dmat = np.where(rel >= 0, np.exp(np.maximum(rel, 0.0)[None] * log_g[:, None, None]), 0.0)
    qdec = np.exp((idx + 1.0)[None, :, None] * log_g[:, None, None])
    kdec = np.exp((t - 1.0 - idx)[None, :, None] * log_g[:, None, None])
    cdec = np.repeat(np.exp(t * log_g), HEAD_DIM).reshape(B_HEADS // 2, LANES, 1)
    dmat, qdec, kdec, cdec = (jnp.asarray(a, F32) for a in (dmat, qdec, kdec, cdec))

    qk_w, v_w = B_HEADS * HEAD_DIM, B_HEADS * B_VDIM
    nb = MIX_BATCH
    seq = lambda n: pl.BlockSpec((nb, t, n), lambda bi, i: (bi, i, 0))
    return pl.pallas_call(
        _retention_kernel,
        grid=(b // nb, lp // t),
        in_specs=[seq(qk_w), seq(qk_w), seq(v_w), seq(v_w),
                  _resident((B_HEADS, t, t)), _resident((B_HEADS, t, 1)),
                  _resident((B_HEADS, t, 1)), _resident((B_HEADS // 2, LANES, 1))],
        out_specs=seq(v_w),
        out_shape=jax.ShapeDtypeStruct((b, lp, v_w), BF16),
        scratch_shapes=[pltpu.VMEM((nb, B_HEADS // 2, LANES, B_VDIM), F32)],
        compiler_params=_params(("parallel", "arbitrary")),
        name="retention",
    )(q, k, v, gate, dmat, qdec, kdec, cdec)


def _swa_kernel(q_ref, k_ref, v_ref, sink_ref, o_ref):
    j = pl.program_id(1)
    nb, tq = q_ref.shape[0], q_ref.shape[1]
    lp = k_ref.shape[1]
    win = tq + C_WIN_CHUNKS * CHUNK
    rep = C_HEADS // C_KV_HEADS
    low_half = _low_half()

    start = pl.multiple_of(jnp.clip(j * tq - C_WIN_CHUNKS * CHUNK, 0, lp - win), CHUNK)
    q_row = j * tq + lax.broadcasted_iota(jnp.int32, (tq, 1), 0)
    q_col = j * tq + lax.broadcasted_iota(jnp.int32, (1, tq), 1)
    q_chunk = jnp.concatenate([q_col // CHUNK] * rep, axis=1)
    key = lax.broadcasted_iota(jnp.int32, (CHUNK + win, 1), 0)
    k_chunk = start // CHUNK + (key - CHUNK) // CHUNK
    in_window = ((key >= CHUNK) & (k_chunk >= 1) & (k_chunk <= q_chunk)
                 & (k_chunk >= q_chunk - C_WIN_CHUNKS))
    valid = in_window | ((key < CHUNK) & (key >= PAD))
    v_row_low = lax.broadcasted_iota(jnp.int32, (LANES, 1), 0) < HEAD_DIM

    chains = [(r, g) for r in range(nb) for g in range(C_KV_HEADS)]
    sinks, scores, values = [], [], []
    for r, g in chains:
        q = jnp.where(q_row < lp, q_ref[r], jnp.zeros_like(q_ref[r]))
        sl = slice(g * LANES, (g + 1) * LANES)
        keys = jnp.concatenate([k_ref[r, 0:CHUNK, sl], k_ref[r, pl.ds(start, win), sl]], axis=0)
        values.append(jnp.concatenate([v_ref[r, 0:CHUNK, sl], v_ref[r, pl.ds(start, win), sl]], axis=0))
        heads = range(g * rep, (g + 1) * rep)
        qs = jnp.concatenate(
            [_keep_head(q[:, (h // 2) * LANES:(h // 2 + 1) * LANES], h % 2, low_half)
             for h in heads], axis=0)
        sinks.append(jnp.concatenate(
            [jnp.broadcast_to(sink_ref[0:1, h:h + 1] * LOG2E, (1, tq)) for h in heads], axis=1))
        scores.append(jnp.where(valid, lax.dot_general(keys, qs, _NT, preferred_element_type=F32), NEG))
    for (r, g), sink, s, vals in zip(chains, sinks, scores, values):
        m = jnp.maximum(jnp.max(s, axis=0, keepdims=True), sink)
        p = jnp.exp2(s - m)
        denom = jnp.sum(p, axis=0, keepdims=True) + jnp.exp2(sink - m)
        y_t = lax.dot_general(vals, p.astype(BF16), _TN, preferred_element_type=F32) / denom
        for u in range(rep // 2):
            out_t = jnp.where(v_row_low, y_t[:, 2 * u * tq:(2 * u + 1) * tq],
                              y_t[:, (2 * u + 1) * tq:(2 * u + 2) * tq])
            pair = g * rep // 2 + u
            o_ref[r, :, pair * LANES:(pair + 1) * LANES] = jnp.where(
                q_row >= PAD, out_t.T, 0.0).astype(BF16)


def _swa(q, k, v, sinks):
    b, lp, width = q.shape
    tq = KEY_TILE
    nb = SWA_BATCH
    kv = pl.BlockSpec((nb, lp, k.shape[2]), lambda bi, j: (bi, 0, 0))
    return pl.pallas_call(
        _swa_kernel,
        grid=(b // nb, -(-lp // tq)),
        in_specs=[pl.BlockSpec((nb, tq, width), lambda bi, j: (bi, j, 0)), kv, kv,
                  _resident((1, C_HEADS))],
        out_specs=pl.BlockSpec((nb, tq, width), lambda bi, j: (bi, j, 0)),
        out_shape=jax.ShapeDtypeStruct((b, lp, width), BF16),
        compiler_params=_params(("parallel", "parallel")),
        name="swa_sinks",
    )(q, k, v, sinks.reshape(1, C_HEADS))


def _pool_kernel(x_ref, mix_ref, scale_ref, o_ref):
    lp = x_ref.shape[1]
    row = lax.broadcasted_iota(jnp.int32, (lp, 1), 0)
    t = row - PAD
    outs = []
    for gi, w in enumerate(POOL_WINDOWS):
        x = x_ref[0, :, gi * D_GROUP:(gi + 1) * D_GROUP]
        s, shift = x, 1
        while shift < w:
            s = s + jnp.where(row >= shift, pltpu.roll(s, shift, 0), 0.0)
            shift *= 2
        count = jnp.maximum(jnp.minimum(t + 1, w), 1).astype(F32)
        y = (s / count - x).astype(BF16)
        outs.append(jnp.dot(y, mix_ref[gi].astype(BF16), preferred_element_type=F32))
    out = jnp.concatenate(outs, axis=-1) * scale_ref[...]
    o_ref[0] = jnp.where(row >= PAD, out, 0.0).astype(BF16)


def _pool(x, mix, scale):
    b, lp, width = x.shape
    n_groups = len(POOL_WINDOWS)
    return pl.pallas_call(
        _pool_kernel,
        grid=(b,),
        in_specs=[pl.BlockSpec((1, lp, width), lambda bi: (bi, 0, 0)),
                  _resident((n_groups, D_GROUP, D_GROUP)), _resident((1, width))],
        out_specs=pl.BlockSpec((1, lp, width), lambda bi: (bi, 0, 0)),
        out_shape=jax.ShapeDtypeStruct((b, lp, width), BF16),
        compiler_params=_params(("parallel",)),
        name="pool_mixer",
    )(x, mix, scale.reshape(1, width))


def kernel(x, meta_tokens, ffn1_norm, ffn1_w_in, ffn1_w_out, mix_norm, ffn2_norm, ffn2_w_in,
           ffn2_w_out, ev_w_in, ev_a_q_norm, ev_a_k_norm, ev_w_out, od_w_in, od_c_q_norm,
           od_c_k_norm, od_c_sinks, od_d_mix, od_d_scale, od_w_out):
    b, s, d = x.shape
    lp = PAD + N_META + s
    assert d == D_MODEL and (b * lp) % ROW_TILE == 0 and s % ROW_TILE == 0
    assert lp % PROJ_TILE == 0 and lp % RET_TILE == 0
    assert min(TOP_K, s // 4) == TOP_K and b % DSA_BATCH == 0 and b % MIX_BATCH == 0 and b % SWA_BATCH == 0
    depth = ffn1_norm.shape[0]
    seq = lambda a: a.reshape(b, lp, a.shape[-1])
    flat = lambda a: a.reshape(b * lp, a.shape[-1])

    ffn1_w_in, ffn1_w_out, ffn2_w_in, ffn2_w_out = (
        w.astype(BF16) for w in (ffn1_w_in, ffn1_w_out, ffn2_w_in, ffn2_w_out))
    for layer in range(depth):
        if layer == 0:
            h = _ffn_first(x, meta_tokens, ffn1_norm[0], ffn1_w_in, ffn1_w_out, 0)
        else:
            h = _ffn(h, ffn1_norm[layer], ffn1_w_in, ffn1_w_out, layer)
        if layer % 2 == 0:
            e = layer // 2
            aq, ak, av, iq, ik, iw, bq, bk, bv, bg = _even_proj(
                seq(h), mix_norm[layer], ev_w_in[e], ev_a_q_norm[e], ev_a_k_norm[e])
            ya = _dsa(aq, ak, av, iq, ik, iw)
            yb = _retention(bq, bk, bv, bg)
            mix = (flat(ya), flat(yb), ev_w_out[e].astype(BF16))
        else:
            o = layer // 2
            cq, ck, cv, dx = _odd_proj(seq(h), mix_norm[layer], od_w_in[o], od_c_q_norm[o],
                                       od_c_k_norm[o])
            yc = _swa(cq, ck, cv, od_c_sinks[o])
            yd = _pool(dx, od_d_mix[o], od_d_scale[o])
            mix = (flat(yc), flat(yd), od_w_out[o].astype(BF16))
        last = layer == depth - 1
        h = _ffn(h, ffn2_norm[layer], ffn2_w_in, ffn2_w_out, layer, (b, s) if last else None, mix)
    return h
```
